```python
import jax, jax.numpy as jnp
from jax import lax
import numpy as np

D_MODEL = 1024
BATCH = 4
SEQ = 4096
DEPTH = 4
DEC_BATCH = 8
DEC_SEQ = 8192
PAST_LEN = 128

CONV_CH = D_MODEL // 4
CONV_W = 3
N_HEADS = 8
N_KV_HEADS = 2
HEAD_DIM = D_MODEL // 16
WINDOW = 128
BLOCK = 128
DN_HEADS = 4
DN_DIM = D_MODEL // 16
DN_CONV_W = 3
CHUNK = 64
D_FF = 4 * D_MODEL
ATT_Q = N_HEADS * HEAD_DIM
ATT_KV = N_KV_HEADS * HEAD_DIM
DN_W = DN_HEADS * DN_DIM
D_MIX = CONV_CH + ATT_Q + DN_W
OFF_ATT = 3 * CONV_CH
OFF_DN = OFF_ATT + ATT_Q + 2 * ATT_KV
D_IN = OFF_DN + 4 * DN_W + 4 * DN_HEADS
ALPHA = (2.0 * DEPTH) ** 0.25
BETA_INIT = (8.0 * DEPTH) ** -0.25
LN_EPS = 1e-5
RMS_EPS = 1e-6

kernel_name = 'hybrid_bidir_conv_swa_gdn_encoder'


def layer_norm(x, g, b):
    xf = x.astype(jnp.float32)
    mu = jnp.mean(xf, -1, keepdims=True)
    var = jnp.mean(jnp.square(xf - mu), -1, keepdims=True)
    y = (xf - mu) * lax.rsqrt(var + LN_EPS) * g.astype(jnp.float32) + b.astype(jnp.float32)
    return y.astype(x.dtype)


def modulate(x, shift, scale):
    return x * (1 + scale[:, None, :]) + shift[:, None, :]


def dwconv_centred(x, w):
    p = w.shape[0] // 2
    return lax.conv_general_dilated(x, w[:, None, :].astype(x.dtype), (1,), ((p, p),),
                                    dimension_numbers=('NWC', 'WIO', 'NWC'),
                                    feature_group_count=x.shape[-1])


def short_conv_mixer(p, conv_w):
    b_gate, c_gate, u = jnp.split(p, 3, axis=-1)
    return b_gate * dwconv_centred(c_gate * u, conv_w)


def window_attention(q, k, v, sink):
    bsz, s, _ = q.shape
    nb = s // BLOCK
    grp = N_HEADS // N_KV_HEADS
    qb = q.reshape(bsz, nb, BLOCK, N_KV_HEADS, grp, HEAD_DIM)

    def neighbours(t):
        tp = jnp.pad(t.reshape(bsz, s, N_KV_HEADS, HEAD_DIM), ((0, 0), (BLOCK, BLOCK), (0, 0), (0, 0)))
        tp = tp.reshape(bsz, nb + 2, BLOCK, N_KV_HEADS, HEAD_DIM)
        return jnp.concatenate([tp[:, :-2], tp[:, 1:-1], tp[:, 2:]], axis=2)

    kb, vb = neighbours(k), neighbours(v)
    scores = jnp.einsum('bnqhgd,bnkhd->bnhgqk', qb, kb).astype(jnp.float32) * HEAD_DIM ** -0.5
    rel = jnp.arange(3 * BLOCK)[None, :] - BLOCK - jnp.arange(BLOCK)[:, None]
    dist = jnp.abs(rel).astype(jnp.float32)
    key_pos = (jnp.arange(nb)[:, None] - 1) * BLOCK + jnp.arange(3 * BLOCK)[None, :]
    valid = (jnp.abs(rel) <= WINDOW)[None] & ((key_pos >= 0) & (key_pos < s))[:, None, :]
    slopes = jnp.exp2(-8.0 * jnp.arange(1, N_HEADS + 1, dtype=jnp.float32) / N_HEADS).reshape(N_KV_HEADS, grp)
    scores = scores - slopes[:, :, None, None] * dist
    scores = jnp.where(valid[:, None, None], scores, -jnp.inf)
    sink_l = sink.astype(jnp.float32).reshape(N_KV_HEADS, grp)[:, :, None, None]
    m = jnp.maximum(jnp.max(scores, -1, keepdims=True), sink_l)
    e = jnp.exp(scores - m)
    denom = jnp.sum(e, -1, keepdims=True) + jnp.exp(sink_l - m)
    probs = (e / denom).astype(v.dtype)
    o = jnp.einsum('bnhgqk,bnkhd->bnqhgd', probs, vb)
    return o.reshape(bsz, s, ATT_Q)


def gated_delta_chunked(q, k, v, g, beta):
    bsz, s, h, dk = q.shape
    n = s // CHUNK

    def chunks(t):
        return jnp.moveaxis(t.reshape(bsz, n, CHUNK, h, *t.shape[3:]), 3, 1)

    q, k, v, g, beta = (chunks(t) for t in (q, k, v, g, beta))
    q = q * dk ** -0.5
    gc = jnp.cumsum(g, axis=-1)
    idx = jnp.arange(CHUNK)
    causal = idx[:, None] >= idx[None, :]
    strict = idx[:, None] > idx[None, :]
    decay = jnp.exp(jnp.where(causal, gc[..., :, None] - gc[..., None, :], -jnp.inf))
    k_beta = k * beta[..., None]
    lower = jnp.where(strict, jnp.einsum('bhncd,bhnsd->bhncs', k_beta, k) * decay, 0.0)
    tmat = lower + jnp.eye(CHUNK, dtype=lower.dtype)
    u = lax.linalg.triangular_solve(tmat, v * beta[..., None], left_side=True, lower=True)
    w = lax.linalg.triangular_solve(tmat, k_beta * jnp.exp(gc)[..., None], left_side=True, lower=True)
    a_intra = jnp.where(causal, jnp.einsum('bhncd,bhnsd->bhncs', q, k) * decay, 0.0)

    def step(state, inp):
        qc, kc, uc, wc, ac, gcc = inp
        v_new = uc - jnp.einsum('bhcd,bhde->bhce', wc, state)
        o = (jnp.einsum('bhcd,bhde->bhce', qc * jnp.exp(gcc)[..., None], state)
             + jnp.einsum('bhcs,bhse->bhce', ac, v_new))
        g_last = gcc[..., -1:]
        state = (state * jnp.exp(g_last)[..., None]
                 + jnp.einsum('bhcd,bhce->bhde', kc * jnp.exp(g_last - gcc)[..., None], v_new))
        return state, o

    xs = tuple(jnp.moveaxis(t, 2, 0) for t in (q, k, u, w, a_intra, gc))
    state0 = jnp.zeros((bsz, h, dk, v.shape[-1]), jnp.float32)
    _, o = lax.scan(step, state0, xs)
    return jnp.transpose(o, (1, 0, 3, 2, 4)).reshape(bsz, s, h, -1)


def l2norm(t):
    return t * lax.rsqrt(jnp.sum(jnp.square(t), -1, keepdims=True) + RMS_EPS)


def delta_mixer(p, conv_w, a_log_f, a_log_b, dt_f, dt_b, norm_g):
    bsz, s, _ = p.shape
    f32 = jnp.float32
    qkv = jax.nn.silu(dwconv_centred(p[..., :3 * DN_W], conv_w)).astype(f32)
    q, k, v = (t.reshape(bsz, s, DN_HEADS, DN_DIM) for t in jnp.split(qkv, 3, axis=-1))
    q, k = l2norm(q), l2norm(k)
    z = p[..., 3 * DN_W:4 * DN_W].astype(f32)
    a_f, a_b, b_f, b_b = jnp.split(p[..., 4 * DN_W:].astype(f32), 4, axis=-1)
    g_f = -jnp.exp(a_log_f.astype(f32)) * jax.nn.softplus(a_f + dt_f.astype(f32))
    g_b = -jnp.exp(a_log_b.astype(f32)) * jax.nn.softplus(a_b + dt_b.astype(f32))
    o_f = gated_delta_chunked(q, k, v, g_f, jax.nn.sigmoid(b_f))
    rev = lambda t: jnp.flip(t, axis=1)
    o_b = rev(gated_delta_chunked(rev(q), rev(k), rev(v), rev(g_b), rev(jax.nn.sigmoid(b_b))))
    o = o_f + o_b
    o = o * lax.rsqrt(jnp.mean(jnp.square(o), -1, keepdims=True) + RMS_EPS) * norm_g.astype(f32)
    return (o.reshape(bsz, s, DN_W) * jax.nn.silu(z)).astype(p.dtype)


def encoder_layer(x, c, w_mod, b_mod, w_in, conv_a_w, attn_sink, dn_conv_w, dn_a_log_f, dn_a_log_b,
                  dn_dt_bias_f, dn_dt_bias_b, dn_norm_g, w_out, ln1_g, ln1_b, w1, b1, w2, b2, ln2_g, ln2_b):
    mod = jax.nn.silu(c) @ w_mod + b_mod
    sh1, sc1, g1, sh2, sc2, g2 = jnp.split(mod, 6, axis=-1)
    h = modulate(x, sh1, sc1)
    p = h @ w_in
    y_a = short_conv_mixer(p[..., :OFF_ATT], conv_a_w)
    q = p[..., OFF_ATT:OFF_ATT + ATT_Q]
    k = p[..., OFF_ATT + ATT_Q:OFF_ATT + ATT_Q + ATT_KV]
    v = p[..., OFF_ATT + ATT_Q + ATT_KV:OFF_DN]
    y_b = window_attention(q, k, v, attn_sink)
    y_c = delta_mixer(p[..., OFF_DN:], dn_conv_w, dn_a_log_f, dn_a_log_b, dn_dt_bias_f, dn_dt_bias_b, dn_norm_g)
    y = jnp.concatenate([y_a, y_b, y_c], axis=-1) @ w_out
    x = layer_norm(ALPHA * x + (1 + g1[:, None, :]) * y, ln1_g, ln1_b)
    h = modulate(x, sh2, sc2)
    f = jnp.square(jax.nn.relu(h @ w1 + b1)) @ w2 + b2
    return layer_norm(ALPHA * x + (1 + g2[:, None, :]) * f, ln2_g, ln2_b)


def trunk(x, c, ln_in_g, ln_in_b, w_mod, b_mod, w_in, conv_a_w, attn_sink, dn_conv_w, dn_a_log_f, dn_a_log_b,
          dn_dt_bias_f, dn_dt_bias_b, dn_norm_g, w_out, ln1_g, ln1_b, w1, b1, w2, b2, ln2_g, ln2_b):
    x = layer_norm(x, ln_in_g, ln_in_b)
    for l in range(DEPTH):
        x = encoder_layer(x, c, w_mod[l], b_mod[l], w_in[l], conv_a_w[l], attn_sink[l], dn_conv_w[l],
                          dn_a_log_f[l], dn_a_log_b[l], dn_dt_bias_f[l], dn_dt_bias_b[l], dn_norm_g[l],
                          w_out[l], ln1_g[l], ln1_b[l], w1[l], b1[l], w2[l], b2[l], ln2_g[l], ln2_b[l])
    return x


def setup_inputs(seed: int = 0) -> dict:
    key = jax.random.key(seed)
    ks = jax.random.split(key, 32)
    f32 = jnp.float32
    L = DEPTH
    nrm = lambda k, shape, scale: jax.random.normal(k, shape, f32) * scale
    return {
        'x_prompt': nrm(ks[0], (BATCH, SEQ, D_MODEL), 1.0),
        'x_sample': nrm(ks[1], (DEC_BATCH, DEC_SEQ, D_MODEL), 1.0),
        'c_prompt': nrm(ks[2], (BATCH, D_MODEL), 1.0),
        'c_sample': nrm(ks[3], (DEC_BATCH, D_MODEL), 1.0),
        'ln_in_g': 1.0 + nrm(ks[4], (D_MODEL,), 0.02),
        'ln_in_b': nrm(ks[5], (D_MODEL,), 0.02),
        'w_mod': nrm(ks[6], (L, D_MODEL, 6 * D_MODEL), 0.2 * D_MODEL ** -0.5),
        'b_mod': nrm(ks[7], (L, 6 * D_MODEL), 0.01),
        'w_in': nrm(ks[8], (L, D_MODEL, D_IN), D_MODEL ** -0.5),
        'conv_a_w': nrm(ks[9], (L, CONV_W, CONV_CH), CONV_W ** -0.5),
        'attn_sink': nrm(ks[10], (L, N_HEADS), 1.0),
        'dn_conv_w': nrm(ks[11], (L, DN_CONV_W, 3 * DN_W), DN_CONV_W ** -0.5),
        'dn_a_log_f': jnp.log(jax.random.uniform(ks[12], (L, DN_HEADS), f32, 1.0, 16.0)),
        'dn_a_log_b': jnp.log(jax.random.uniform(ks[13], (L, DN_HEADS), f32, 1.0, 16.0)),
        'dn_dt_bias_f': jnp.log(jnp.expm1(jax.random.uniform(ks[14], (L, DN_HEADS), f32, 1e-3, 1e-1))),
        'dn_dt_bias_b': jnp.log(jnp.expm1(jax.random.uniform(ks[15], (L, DN_HEADS), f32, 1e-3, 1e-1))),
        'dn_norm_g': 1.0 + nrm(ks[16], (L, DN_DIM), 0.02),
        'w_out': nrm(ks[17], (L, D_MIX, D_MODEL), BETA_INIT * D_MIX ** -0.5),
        'ln1_g': 1.0 + nrm(ks[18], (L, D_MODEL), 0.02),
        'ln1_b': nrm(ks[19], (L, D_MODEL), 0.02),
        'w1': nrm(ks[20], (L, D_MODEL, D_FF), D_MODEL ** -0.5),
        'b1': nrm(ks[21], (L, D_FF), 0.01),
        'w2': nrm(ks[22], (L, D_FF, D_MODEL), BETA_INIT * D_FF ** -0.5),
        'b2': nrm(ks[23], (L, D_MODEL), 0.01),
        'ln2_g': 1.0 + nrm(ks[24], (L, D_MODEL), 0.02),
        'ln2_b': nrm(ks[25], (L, D_MODEL), 0.02),
    }


def reference(x_prompt, x_sample, c_prompt, c_sample, ln_in_g, ln_in_b, w_mod, b_mod, w_in, conv_a_w, attn_sink,
              dn_conv_w, dn_a_log_f, dn_a_log_b, dn_dt_bias_f, dn_dt_bias_b, dn_norm_g, w_out, ln1_g, ln1_b,
              w1, b1, w2, b2, ln2_g, ln2_b):
    params = (ln_in_g, ln_in_b, w_mod, b_mod, w_in, conv_a_w, attn_sink, dn_conv_w, dn_a_log_f, dn_a_log_b,
              dn_dt_bias_f, dn_dt_bias_b, dn_norm_g, w_out, ln1_g, ln1_b, w1, b1, w2, b2, ln2_g, ln2_b)
    y_prompt = trunk(x_prompt, c_prompt, *params)
    y_sample = trunk(x_sample, c_sample, *params)
    return (y_prompt, y_sample)
```

```python
import functools

import jax
import jax.numpy as jnp
from jax import lax
from jax.experimental import pallas as pl
from jax.experimental.pallas import tpu as pltpu

F32 = jnp.float32
BF16 = jnp.bfloat16

D_MODEL = 1024
DEPTH = 4
CONV_CH = 256
N_HEADS = 8
N_KV_HEADS = 2
HEAD_DIM = 64
WINDOW = 128
DN_HEADS = 4
DN_DIM = 64
DN_W = DN_HEADS * DN_DIM
D_FF = 4 * D_MODEL
ATT_Q = N_HEADS * HEAD_DIM
ATT_KV = N_KV_HEADS * HEAD_DIM
OFF_ATT = 3 * CONV_CH
OFF_DN = OFF_ATT + ATT_Q + 2 * ATT_KV
OFF_GATE = OFF_DN + 4 * DN_W
D_IN = OFF_GATE + 4 * DN_HEADS
ALPHA = (2.0 * DEPTH) ** 0.25
LN_EPS = 1e-5
RMS_EPS = 1e-6

LANES = 128
BF16_SUBLANES = 16
F32_SUBLANES = 8
VMEM_LIMIT_BYTES = 56 * 1024 * 1024

ROW_TILE = 512
ATT_TILE = 512
ATT_BLOCK = 128
DN_TILE = 256
CHUNK = 64
FF_CHUNK = 1024


def _dot(a, b):
    return jnp.dot(a, b, preferred_element_type=F32)


def _dot_nt(a, b):
    return lax.dot_general(a, b, (((1,), (1,)), ((), ())), preferred_element_type=F32)


def _dot_tn(a, b):
    return lax.dot_general(a, b, (((0,), (0,)), ((), ())), preferred_element_type=F32)


def _sigmoid(x):
    return 1.0 / (1.0 + jnp.exp(-x))


def _silu(x):
    return x * _sigmoid(x)


def _softplus(x):
    return jnp.maximum(x, 0.0) + jnp.log(1.0 + jnp.exp(-jnp.abs(x)))


def _layer_norm(x, g, b):
    mu = jnp.mean(x, axis=-1, keepdims=True)
    xc = x - mu
    var = jnp.mean(xc * xc, axis=-1, keepdims=True)
    return xc * lax.rsqrt(var + LN_EPS) * g + b


def _split3(x):
    hi = x.astype(BF16)
    r1 = x - hi.astype(F32)
    mid = r1.astype(BF16)
    lo = (r1 - mid.astype(F32)).astype(BF16)
    return hi, mid, lo


def _dot_exact_rhs(sel, x):
    hi, mid, lo = _split3(x)
    return _dot(sel, hi) + _dot(sel, mid) + _dot(sel, lo)


def _dot_exact_lhs(x, sel):
    hi, mid, lo = _split3(x)
    return _dot(hi, sel) + _dot(mid, sel) + _dot(lo, sel)


def _iota2(shape, axis):
    return lax.broadcasted_iota(jnp.int32, shape, axis)


def _mod_kernel(c_ref, w_ref, b_ref, o_ref):
    s = _silu(c_ref[...]).astype(BF16)
    o_ref[0] = _dot(s, w_ref[0].astype(BF16)) + b_ref[0]


def _modulation(c_all, w_mod, b_mod):
    bp = c_all.shape[0]
    nblk = w_mod.shape[2] // D_MODEL
    return pl.pallas_call(
        _mod_kernel,
        grid=(DEPTH, nblk),
        in_specs=[
            pl.BlockSpec((bp, D_MODEL), lambda l, k: (0, 0)),
            pl.BlockSpec((1, D_MODEL, D_MODEL), lambda l, k: (l, 0, k)),
            pl.BlockSpec((1, 1, D_MODEL), lambda l, k: (l, 0, k)),
        ],
        out_specs=pl.BlockSpec((1, bp, D_MODEL), lambda l, k: (l, 0, k)),
        out_shape=jax.ShapeDtypeStruct((DEPTH, bp, 6 * D_MODEL), F32),
        compiler_params=pltpu.CompilerParams(dimension_semantics=("arbitrary", "arbitrary"),
                                             vmem_limit_bytes=VMEM_LIMIT_BYTES),
        name="modulation",
    )(c_all, w_mod, b_mod.reshape(DEPTH, 1, 6 * D_MODEL))


def _inproj_kernel(apply_ln, x_ref, sh_ref, sc_ref, lng_ref, lnb_ref, w_ref, wg_ref, *out_refs):
    x = x_ref[0]
    if apply_ln:
        x = _layer_norm(x, lng_ref[...], lnb_ref[...])
        out_refs[6][0] = x
    h = (x * (1.0 + sc_ref[0]) + sh_ref[0]).astype(BF16)
    p = _dot(h, w_ref[...])
    o_conv, o_q, o_kv, o_dn, o_z, o_g = out_refs[:6]
    o_conv[0] = p[:, :OFF_ATT].astype(BF16)
    o_q[0] = (p[:, OFF_ATT:OFF_ATT + ATT_Q] * (HEAD_DIM ** -0.5)).astype(BF16)
    o_kv[0] = p[:, OFF_ATT + ATT_Q:OFF_DN].astype(BF16)
    o_dn[0] = p[:, OFF_DN:OFF_DN + 3 * DN_W].astype(BF16)
    o_z[0] = p[:, OFF_DN + 3 * DN_W:OFF_GATE].astype(BF16)
    o_g[0] = _dot(h, wg_ref[...])


def _in_projection(x, mod_l, b0, ln_g, ln_b, w_main, w_gate, apply_ln):
    bsz, s, _ = x.shape
    tm = min(ROW_TILE, s)
    row = lambda b, i: (b, i, 0)
    full2 = lambda b, i: (0, 0)
    widths = (OFF_ATT, ATT_Q, 2 * ATT_KV, 3 * DN_W, DN_W, LANES)
    dtypes = (BF16, BF16, BF16, BF16, BF16, F32)
    out_shape = [jax.ShapeDtypeStruct((bsz, s, w), dt) for w, dt in zip(widths, dtypes)]
    out_specs = [pl.BlockSpec((1, tm, w), row) for w in widths]
    if apply_ln:
        out_shape.append(jax.ShapeDtypeStruct((bsz, s, D_MODEL), F32))
        out_specs.append(pl.BlockSpec((1, tm, D_MODEL), row))
    return pl.pallas_call(
        functools.partial(_inproj_kernel, apply_ln),
        grid=(bsz, s // tm),
        in_specs=[
            pl.BlockSpec((1, tm, D_MODEL), row),
            pl.BlockSpec((1, 1, D_MODEL), lambda b, i: (b0 + b, 0, 0)),
            pl.BlockSpec((1, 1, D_MODEL), lambda b, i: (b0 + b, 0, 1)),
            pl.BlockSpec((1, D_MODEL), full2),
            pl.BlockSpec((1, D_MODEL), full2),
            pl.BlockSpec((D_MODEL, OFF_GATE), full2, pipeline_mode=pl.Buffered(1)),
            pl.BlockSpec((D_MODEL, LANES), full2, pipeline_mode=pl.Buffered(1)),
        ],
        out_specs=out_specs,
        out_shape=out_shape,
        compiler_params=pltpu.CompilerParams(dimension_semantics=("parallel", "parallel"),
                                             vmem_limit_bytes=VMEM_LIMIT_BYTES),
        name="in_projection",
    )(x, mod_l, mod_l, ln_g, ln_b, w_main, w_gate)


def _mix_ab_kernel(seq_len, pc_ref, pc_prev_ref, pc_next_ref, q_ref, kv_ref, kv_prev_ref, kv_next_ref,
                   cw_ref, sink_ref, o_ref, cu_ext, ka_ref, kb_ref, va_ref, vb_ref):
    i = pl.program_id(1)
    n = pl.num_programs(1)
    tq = q_ref.shape[1]
    has_prev = (i > 0).astype(F32)
    has_next = (i < n - 1).astype(F32)

    pc = pc_ref[0]
    bg = pc[:, :CONV_CH].astype(F32)
    cu = pc[:, CONV_CH:2 * CONV_CH].astype(F32) * pc[:, 2 * CONV_CH:].astype(F32)
    pp = pc_prev_ref[0].astype(F32)[BF16_SUBLANES - F32_SUBLANES:, :]
    pn = pc_next_ref[0].astype(F32)[:F32_SUBLANES, :]
    cu_ext[0:F32_SUBLANES, :] = (pp[:, CONV_CH:2 * CONV_CH] * pp[:, 2 * CONV_CH:]) * has_prev
    cu_ext[F32_SUBLANES:F32_SUBLANES + tq, :] = cu
    cu_ext[F32_SUBLANES + tq:, :] = (pn[:, CONV_CH:2 * CONV_CH] * pn[:, 2 * CONV_CH:]) * has_next
    cw = cw_ref[...]
    conv = (cw[0:1, :] * cu_ext[F32_SUBLANES - 1:F32_SUBLANES - 1 + tq, :] + cw[1:2, :] * cu
            + cw[2:3, :] * cu_ext[F32_SUBLANES + 1:F32_SUBLANES + 1 + tq, :])
    o_ref[0, :, :CONV_CH] = (bg * conv).astype(BF16)

    def fill(r0, blk):
        kk = blk[:, :LANES].astype(F32)
        vv = blk[:, LANES:].astype(F32)
        lo = _iota2(kk.shape, 1) < HEAD_DIM
        kks = pltpu.roll(kk, HEAD_DIM, 1)
        vvs = pltpu.roll(vv, HEAD_DIM, 1)
        rows = pl.ds(r0, blk.shape[0])
        ka_ref[0, rows, :] = jnp.where(lo, kk, 0.0).astype(BF16)
        kb_ref[0, rows, :] = jnp.where(lo, 0.0, kks).astype(BF16)
        ka_ref[1, rows, :] = jnp.where(lo, kks, 0.0).astype(BF16)
        kb_ref[1, rows, :] = jnp.where(lo, 0.0, kk).astype(BF16)
        va_ref[0, rows, :] = jnp.where(lo, vv, 0.0).astype(BF16)
        vb_ref[0, rows, :] = jnp.where(lo, 0.0, vvs).astype(BF16)
        va_ref[1, rows, :] = jnp.where(lo, vvs, 0.0).astype(BF16)
        vb_ref[1, rows, :] = jnp.where(lo, 0.0, vv).astype(BF16)

    fill(0, kv_prev_ref[0])
    fill(ATT_BLOCK, kv_ref[0])
    fill(ATT_BLOCK + tq, kv_next_ref[0])

    nwin = 3 * ATT_BLOCK
    qi = _iota2((ATT_BLOCK, nwin), 0)
    ki = _iota2((ATT_BLOCK, nwin), 1)
    rel = ki - ATT_BLOCK - qi
    dist = jnp.abs(rel).astype(F32)
    lane_lo = _iota2((ATT_BLOCK, LANES), 1) < HEAD_DIM
    grp = N_HEADS // N_KV_HEADS

    for j in range(tq // ATT_BLOCK):
        kpos = i * tq + (j - 1) * ATT_BLOCK + ki
        valid = (jnp.abs(rel) <= WINDOW) & (kpos >= 0) & (kpos < seq_len)
        win = pl.ds(j * ATT_BLOCK, nwin)
        qrows = pl.ds(j * ATT_BLOCK, ATT_BLOCK)
        for hk in range(N_KV_HEADS):
            kas, kbs = ka_ref[hk, win, :], kb_ref[hk, win, :]
            vas, vbs = va_ref[hk, win, :], vb_ref[hk, win, :]
            for pr in range(grp // 2):
                pair = hk * (grp // 2) + pr
                qp = q_ref[0, qrows, pair * LANES:(pair + 1) * LANES]
                acc = None
                rcp = []
                for par, kz, vz in ((0, kas, vas), (1, kbs, vbs)):
                    head = 2 * pair + par
                    slope = 2.0 ** (-8.0 * (head + 1) / N_HEADS)
                    sc = _dot_nt(qp, kz) - slope * dist
                    sc = jnp.where(valid, sc, -jnp.inf)
                    sink = sink_ref[head]
                    m = jnp.maximum(jnp.max(sc, axis=-1, keepdims=True), sink)
                    e = jnp.exp(sc - m)
                    denom = jnp.sum(e, axis=-1, keepdims=True) + jnp.exp(sink - m)
                    rcp.append(1.0 / denom)
                    pv = _dot(e.astype(BF16), vz)
                    acc = pv if acc is None else acc + pv
                out = acc * jnp.where(lane_lo, rcp[0], rcp[1])
                o_ref[0, qrows, CONV_CH + pair * LANES:CONV_CH + (pair + 1) * LANES] = out.astype(BF16)


def _mix_ab(pconv, q, kv, conv_w, sink):
    bsz, s, _ = q.shape
    tq = min(ATT_TILE, s)
    nh = tq // BF16_SUBLANES
    nb = tq // ATT_BLOCK
    row = lambda b, i: (b, i, 0)
    return pl.pallas_call(
        functools.partial(_mix_ab_kernel, s),
        grid=(bsz, s // tq),
        in_specs=[
            pl.BlockSpec((1, tq, OFF_ATT), row),
            pl.BlockSpec((1, BF16_SUBLANES, OFF_ATT), lambda b, i: (b, jnp.maximum(i * nh - 1, 0), 0)),
            pl.BlockSpec((1, BF16_SUBLANES, OFF_ATT),
                         lambda b, i: (b, jnp.minimum((i + 1) * nh, s // BF16_SUBLANES - 1), 0)),
            pl.BlockSpec((1, tq, ATT_Q), row),
            pl.BlockSpec((1, tq, 2 * ATT_KV), row),
            pl.BlockSpec((1, ATT_BLOCK, 2 * ATT_KV), lambda b, i: (b, jnp.maximum(i * nb - 1, 0), 0)),
            pl.BlockSpec((1, ATT_BLOCK, 2 * ATT_KV),
                         lambda b, i: (b, jnp.minimum((i + 1) * nb, s // ATT_BLOCK - 1), 0)),
            pl.BlockSpec((F32_SUBLANES, CONV_CH), lambda b, i: (0, 0)),
            pl.BlockSpec(memory_space=pltpu.SMEM),
        ],
        out_specs=pl.BlockSpec((1, tq, CONV_CH + ATT_Q), row),
        out_shape=jax.ShapeDtypeStruct((bsz, s, CONV_CH + ATT_Q), BF16),
        scratch_shapes=[
            pltpu.VMEM((tq + 2 * F32_SUBLANES, CONV_CH), F32),
            pltpu.VMEM((N_KV_HEADS, tq + 2 * ATT_BLOCK, LANES), BF16),
            pltpu.VMEM((N_KV_HEADS, tq + 2 * ATT_BLOCK, LANES), BF16),
            pltpu.VMEM((N_KV_HEADS, tq + 2 * ATT_BLOCK, LANES), BF16),
            pltpu.VMEM((N_KV_HEADS, tq + 2 * ATT_BLOCK, LANES), BF16),
        ],
        compiler_params=pltpu.CompilerParams(dimension_semantics=("parallel", "parallel"),
                                             vmem_limit_bytes=VMEM_LIMIT_BYTES),
        name="conv_window_attention",
    )(pconv, pconv, pconv, q, kv, kv, kv, conv_w, sink)


def _head_of(idx):
    return lax.shift_right_logical(idx, 6)


def _pos_in_head(idx):
    return lax.bitwise_and(idx, DN_DIM - 1)


def _bd(y, mask_bd):
    return jnp.where(mask_bd, jnp.concatenate([y] * DN_HEADS, axis=0), jnp.zeros((), y.dtype))


def _mm(a, y, mask_bd):
    return _dot(a.astype(BF16), _bd(y.astype(BF16), mask_bd))


def _mm_nt(a, y, mask_bd):
    return _dot_nt(a.astype(BF16), _bd(y.astype(BF16), mask_bd))


def _dn_chunk(reverse, q, k, v, gcol, bcol, st, masks):
    mask_bd, eye_p, tri_incl, tri_strict, level_masks, eye_bd = masks
    grow = jnp.sum(jnp.where(eye_p, gcol, 0.0), axis=0, keepdims=True)
    gtot = gcol[0:1, :] if reverse else gcol[CHUNK - 1:CHUNK, :]
    decay = jnp.exp(jnp.where(tri_incl, gcol - grow, -jnp.inf))
    kq = jnp.concatenate([k, q], axis=0)
    gq = _mm_nt(kq, k, mask_bd)
    lmat = jnp.where(tri_strict, gq[:CHUNK] * bcol * decay, 0.0)
    amat = jnp.where(tri_incl, gq[CHUNK:] * decay, 0.0)
    x = jnp.where(eye_p, 1.0, 0.0) - jnp.where(level_masks[0], lmat, 0.0)
    for lm in level_masks[1:]:
        z = _mm(jnp.where(lm, lmat, 0.0), x, mask_bd)
        x = x - _mm(x, z, mask_bd)
    egc = jnp.exp(gcol)
    u = _mm(x, v * bcol, mask_bd)
    w = _mm(x, k * (bcol * egc), mask_bd)
    p = q * egc - _mm(amat, w, mask_bd)
    r = _mm(amat, u, mask_bd)
    o = _mm_nt(p, st, mask_bd) + r
    kd = (k * jnp.exp(gtot - gcol)).astype(BF16)
    wu = jnp.concatenate([w, u], axis=1).astype(BF16)
    full = _dot_tn(wu, kd)
    bdm = jnp.where(eye_bd, jnp.exp(gtot), 0.0) - jnp.where(mask_bd, full[:DN_W], 0.0)
    nfull = jnp.where(mask_bd, full[DN_W:], 0.0)
    nt = nfull[0:64] + nfull[64:128] + nfull[128:192] + nfull[192:256]
    st_new = _dot(st.astype(BF16), bdm.astype(BF16)) + nt
    return o, st_new


def _dn_masks(reverse):
    r2 = _iota2((DN_W, DN_W), 0)
    c2 = _iota2((DN_W, DN_W), 1)
    mask_bd = _head_of(r2) == _head_of(c2)
    eye_bd = r2 == c2
    i = _iota2((CHUNK, DN_W), 0)
    j = _pos_in_head(_iota2((CHUNK, DN_W), 1))
    eye_p = i == j
    if reverse:
        tri_incl, tri_strict = i <= j, i < j
    else:
        tri_incl, tri_strict = i >= j, i > j
    level_masks = []
    for lvl in range(1, 7):
        half = 1 << (lvl - 1)
        same_blk = lax.shift_right_logical(i, lvl) == lax.shift_right_logical(j, lvl)
        i_hi = lax.bitwise_and(i, half) != 0
        j_hi = lax.bitwise_and(j, half) != 0
        if reverse:
            level_masks.append(same_blk & jnp.logical_not(i_hi) & j_hi)
        else:
            level_masks.append(same_blk & i_hi & jnp.logical_not(j_hi))
    return mask_bd, eye_p, tri_incl, tri_strict, level_masks, eye_bd


def _dn_kernel(reverse, x_ref, x_prev_ref, x_next_ref, g_ref, cw_ref, gp_ref, *rest):
    if reverse:
        z_ref, of_ref, ng_ref, o_ref, ext, st_ref = rest
    else:
        o_ref, ext, st_ref = rest
    i = pl.program_id(1)
    n = pl.num_programs(1)
    t = (n - 1 - i) if reverse else i
    ts = x_ref.shape[1]
    has_prev = (t > 0).astype(F32)
    has_next = (t < n - 1).astype(F32)

    @pl.when(i == 0)
    def _():
        st_ref[...] = jnp.zeros(st_ref.shape, F32)

    ext[0:F32_SUBLANES, :] = x_prev_ref[0].astype(F32)[BF16_SUBLANES - F32_SUBLANES:, :] * has_prev
    ext[F32_SUBLANES:F32_SUBLANES + ts, :] = x_ref[0].astype(F32)
    ext[F32_SUBLANES + ts:, :] = x_next_ref[0].astype(F32)[:F32_SUBLANES, :] * has_next
    cw = cw_ref[...]
    qkv = _silu(cw[0:1, :] * ext[F32_SUBLANES - 1:F32_SUBLANES - 1 + ts, :]
                + cw[1:2, :] * ext[F32_SUBLANES:F32_SUBLANES + ts, :]
                + cw[2:3, :] * ext[F32_SUBLANES + 1:F32_SUBLANES + 1 + ts, :])
    masks = _dn_masks(reverse)
    mask_bd = masks[0]
    ones_bd = jnp.where(mask_bd, 1.0, 0.0).astype(BF16)

    def head_sumsq(y):
        y2 = y * y
        hi = y2.astype(BF16)
        lo = (y2 - hi.astype(F32)).astype(BF16)
        return _dot(hi, ones_bd) + _dot(lo, ones_bd)

    q = qkv[:, :DN_W]
    k = qkv[:, DN_W:2 * DN_W]
    v = qkv[:, 2 * DN_W:]
    q = q * (lax.rsqrt(head_sumsq(q) + RMS_EPS) * (DN_DIM ** -0.5))
    k = k * lax.rsqrt(head_sumsq(k) + RMS_EPS)

    gates = g_ref[0]
    gp = gp_ref[...]
    lane = _iota2(gates.shape, 1)
    glog = -gp[0:1, :] * _softplus(gates + gp[1:2, :])
    beta = _sigmoid(gates)
    ri = _iota2((ts, ts), 0)
    ci = _iota2((ts, ts), 1)
    same_chunk = _head_of(ri) == _head_of(ci)
    cum_sel = jnp.where(same_chunk & ((ci >= ri) if reverse else (ci <= ri)), 1.0, 0.0).astype(BF16)
    gcum = _dot_exact_rhs(cum_sel, glog)
    goff, boff = (DN_HEADS, 3 * DN_HEADS) if reverse else (0, 2 * DN_HEADS)
    er = _iota2((LANES, DN_W), 0)
    ec = _head_of(_iota2((LANES, DN_W), 1))
    gcol_all = _dot_exact_lhs(gcum, jnp.where(er == ec + goff, 1.0, 0.0).astype(BF16))
    bcol_all = _dot_exact_lhs(beta, jnp.where(er == ec + boff, 1.0, 0.0).astype(BF16))

    if reverse:
        ng = ng_ref[...]
    st = st_ref[...]
    nchunk = ts // CHUNK
    order = range(nchunk - 1, -1, -1) if reverse else range(nchunk)
    for c in order:
        rows = slice(c * CHUNK, (c + 1) * CHUNK)
        o, st = _dn_chunk(reverse, q[rows], k[rows], v[rows], gcol_all[rows], bcol_all[rows], st, masks)
        if reverse:
            o = o + of_ref[0, rows, :]
            ms = head_sumsq(o) * (1.0 / DN_DIM)
            y = o * lax.rsqrt(ms + RMS_EPS) * ng * _silu(z_ref[0, rows, :].astype(F32))
            o_ref[0, rows, :] = y.astype(BF16)
        else:
            o_ref[0, rows, :] = o
    st_ref[...] = st


def _delta_pass(reverse, dnqkv, gates, conv_w, gate_params, z=None, o_fwd=None, norm_g=None):
    bsz, s, _ = dnqkv.shape
    ts = min(DN_TILE, s)
    n = s // ts
    nh = ts // BF16_SUBLANES
    pos = (lambda i: n - 1 - i) if reverse else (lambda i: i)
    row = lambda b, i: (b, pos(i), 0)
    full2 = lambda b, i: (0, 0)
    in_specs = [
        pl.BlockSpec((1, ts, 3 * DN_W), row),
        pl.BlockSpec((1, BF16_SUBLANES, 3 * DN_W), lambda b, i: (b, jnp.maximum(pos(i) * nh - 1, 0), 0)),
        pl.BlockSpec((1, BF16_SUBLANES, 3 * DN_W),
                     lambda b, i: (b, jnp.minimum((pos(i) + 1) * nh, s // BF16_SUBLANES - 1), 0)),
        pl.BlockSpec((1, ts, LANES), row),
        pl.BlockSpec((F32_SUBLANES, 3 * DN_W), full2),
        pl.BlockSpec((F32_SUBLANES, LANES), full2),
    ]
    args = [dnqkv, dnqkv, dnqkv, gates, conv_w, gate_params]
    if reverse:
        in_specs += [pl.BlockSpec((1, ts, DN_W), row), pl.BlockSpec((1, ts, DN_W), row),
                     pl.BlockSpec((1, DN_W), full2)]
        args += [z, o_fwd, norm_g]
        out_dtype = BF16
    else:
        out_dtype = F32
    return pl.pallas_call(
        functools.partial(_dn_kernel, reverse),
        grid=(bsz, n),
        in_specs=in_specs,
        out_specs=pl.BlockSpec((1, ts, DN_W), row),
        out_shape=jax.ShapeDtypeStruct((bsz, s, DN_W), out_dtype),
        scratch_shapes=[pltpu.VMEM((ts + 2 * F32_SUBLANES, 3 * DN_W), F32),
                        pltpu.VMEM((DN_DIM, DN_W), F32)],
        compiler_params=pltpu.CompilerParams(dimension_semantics=("parallel", "arbitrary"),
                                             vmem_limit_bytes=VMEM_LIMIT_BYTES),
        name="delta_bwd" if reverse else "delta_fwd",
    )(*args)


def _out_mlp_kernel(x_ref, yab_ref, yc_ref, g1_ref, sh2_ref, sc2_ref, g2_ref, wo_ab_ref, wo_c_ref,
                    ln1g_ref, ln1b_ref, w1_ref, b1_ref, w2_ref, b2_ref, ln2g_ref, ln2b_ref, o_ref):
    x = x_ref[0]
    y = _dot(yab_ref[0], wo_ab_ref[...]) + _dot(yc_ref[0], wo_c_ref[...])
    x1 = _layer_norm(ALPHA * x + (1.0 + g1_ref[0]) * y, ln1g_ref[...], ln1b_ref[...])
    h = (x1 * (1.0 + sc2_ref[0]) + sh2_ref[0]).astype(BF16)
    f = None
    for c in range(D_FF // FF_CHUNK):
        cols = slice(c * FF_CHUNK, (c + 1) * FF_CHUNK)
        a = jnp.maximum(_dot(h, w1_ref[:, cols]) + b1_ref[:, cols], 0.0)
        part = _dot((a * a).astype(BF16), w2_ref[cols, :])
        f = part if f is None else f + part
    f = f + b2_ref[...]
    o_ref[0] = _layer_norm(ALPHA * x1 + (1.0 + g2_ref[0]) * f, ln2g_ref[...], ln2b_ref[...])


def _out_mlp(x, yab, yc, mod_l, b0, wo_ab, wo_c, ln1g, ln1b, w1, b1, w2, b2, ln2g, ln2b):
    bsz, s, _ = x.shape
    tm = min(ROW_TILE, s)
    row = lambda b, i: (b, i, 0)
    full2 = lambda b, i: (0, 0)
    modspec = lambda k: pl.BlockSpec((1, 1, D_MODEL), lambda b, i: (b0 + b, 0, k))
    const = lambda shape: pl.BlockSpec(shape, full2, pipeline_mode=pl.Buffered(1))
    return pl.pallas_call(
        _out_mlp_kernel,
        grid=(bsz, s // tm),
        in_specs=[
            pl.BlockSpec((1, tm, D_MODEL), row),
            pl.BlockSpec((1, tm, CONV_CH + ATT_Q), row),
            pl.BlockSpec((1, tm, DN_W), row),
            modspec(2), modspec(3), modspec(4), modspec(5),
            const((CONV_CH + ATT_Q, D_MODEL)), const((DN_W, D_MODEL)),
            const((1, D_MODEL)), const((1, D_MODEL)),
            const((D_MODEL, D_FF)), const((1, D_FF)),
            const((D_FF, D_MODEL)), const((1, D_MODEL)),
            const((1, D_MODEL)), const((1, D_MODEL)),
        ],
        out_specs=pl.BlockSpec((1, tm, D_MODEL), row),
        out_shape=jax.ShapeDtypeStruct((bsz, s, D_MODEL), F32),
        compiler_params=pltpu.CompilerParams(dimension_semantics=("parallel", "parallel"),
                                             vmem_limit_bytes=VMEM_LIMIT_BYTES),
        name="out_mlp",
    )(x, yab, yc, mod_l, mod_l, mod_l, mod_l, wo_ab, wo_c, ln1g, ln1b, w1, b1, w2, b2, ln2g, ln2b)


def _pad_rows(a, rows):
    return jnp.concatenate([a, jnp.zeros((rows - a.shape[0],) + a.shape[1:], a.dtype)], axis=0)


def _trunk(x, mod, b0, prm):
    for l in range(DEPTH):
        mod_l = mod[l]
        outs = _in_projection(x, mod_l, b0, prm["ln_in_g"], prm["ln_in_b"], prm["w_main"][l], prm["w_gate"][l],
                              apply_ln=(l == 0))
        pconv, q, kv, dnqkv, z, gates = outs[:6]
        if l == 0:
            x = outs[6]
        yab = _mix_ab(pconv, q, kv, prm["conv_a_w"][l], prm["attn_sink"][l])
        o_fwd = _delta_pass(False, dnqkv, gates, prm["dn_conv_w"][l], prm["gate_params"][l])
        yc = _delta_pass(True, dnqkv, gates, prm["dn_conv_w"][l], prm["gate_params"][l],
                         z=z, o_fwd=o_fwd, norm_g=prm["dn_norm_g"][l])
        x = _out_mlp(x, yab, yc, mod_l, b0, prm["wo_ab"][l], prm["wo_c"][l], prm["ln1_g"][l], prm["ln1_b"][l],
                     prm["w1"][l], prm["b1"][l], prm["w2"][l], prm["b2"][l], prm["ln2_g"][l], prm["ln2_b"][l])
    return x


def kernel(x_prompt, x_sample, c_prompt, c_sample, ln_in_g, ln_in_b, w_mod, b_mod, w_in, conv_a_w, attn_sink,
           dn_conv_w, dn_a_log_f, dn_a_log_b, dn_dt_bias_f, dn_dt_bias_b, dn_norm_g, w_out, ln1_g, ln1_b,
           w1, b1, w2, b2, ln2_g, ln2_b):
    nb_p, nb_s = c_prompt.shape[0], c_sample.shape[0]
    bp = -(-(nb_p + nb_s) // BF16_SUBLANES) * BF16_SUBLANES
    c_all = _pad_rows(jnp.concatenate([c_prompt, c_sample], axis=0), bp)
    mod = _modulation(c_all, w_mod, b_mod)
    mod = mod.reshape(DEPTH, bp, 1, 6 * D_MODEL)

    row = lambda a: a.reshape(DEPTH, 1, -1)
    zeros4 = jnp.zeros((DEPTH, DN_HEADS), F32)
    gp0 = jnp.concatenate([jnp.exp(dn_a_log_f), jnp.exp(dn_a_log_b), zeros4, zeros4], axis=1)
    gp1 = jnp.concatenate([dn_dt_bias_f, dn_dt_bias_b, zeros4, zeros4], axis=1)
    gate_params = jnp.stack([gp0, gp1], axis=1)
    gate_params = jnp.pad(gate_params, ((0, 0), (0, F32_SUBLANES - 2), (0, LANES - 4 * DN_HEADS)))
    pad_taps = lambda w: jnp.pad(w, ((0, 0), (0, F32_SUBLANES - w.shape[1]), (0, 0)))
    prm = dict(
        ln_in_g=ln_in_g.reshape(1, -1), ln_in_b=ln_in_b.reshape(1, -1),
        w_main=w_in[:, :, :OFF_GATE].astype(BF16),
        w_gate=jnp.pad(w_in[:, :, OFF_GATE:], ((0, 0), (0, 0), (0, LANES - 4 * DN_HEADS))).astype(BF16),
        conv_a_w=pad_taps(conv_a_w), attn_sink=attn_sink,
        dn_conv_w=pad_taps(dn_conv_w), gate_params=gate_params,
        dn_norm_g=jnp.tile(dn_norm_g, (1, DN_HEADS)).reshape(DEPTH, 1, DN_W),
        wo_ab=w_out[:, :CONV_CH + ATT_Q, :].astype(BF16), wo_c=w_out[:, CONV_CH + ATT_Q:, :].astype(BF16),
        ln1_g=row(ln1_g), ln1_b=row(ln1_b), w1=w1.astype(BF16), b1=row(b1), w2=w2.astype(BF16), b2=row(b2),
        ln2_g=row(ln2_g), ln2_b=row(ln2_b),
    )
    y_prompt = _trunk(x_prompt, mod, 0, prm)
    y_sample = _trunk(x_sample, mod, nb_p, prm)
    return (y_prompt, y_sample)
```

```python
import functools

import numpy as np
import jax
import jax.numpy as jnp
from jax import lax
from jax.experimental import pallas as pl
from jax.experimental.pallas import tpu as pltpu

F32 = jnp.float32
BF16 = jnp.bfloat16

D_MODEL = 1024
DEPTH = 4
CONV_CH = 256
N_HEADS = 8
N_KV_HEADS = 2
HEAD_DIM = 64
WINDOW = 128
DN_HEADS = 4
DN_DIM = 64
DN_W = DN_HEADS * DN_DIM
D_FF = 4 * D_MODEL
ATT_Q = N_HEADS * HEAD_DIM
ATT_KV = N_KV_HEADS * HEAD_DIM
OFF_ATT = 3 * CONV_CH
OFF_DN = OFF_ATT + ATT_Q + 2 * ATT_KV
OFF_GATE = OFF_DN + 4 * DN_W
D_IN = OFF_GATE + 4 * DN_HEADS
ALPHA = (2.0 * DEPTH) ** 0.25
LN_EPS = 1e-5
RMS_EPS = 1e-6

LANES = 128
BF16_SUBLANES = 16
F32_SUBLANES = 8
VMEM_LIMIT_BYTES = 56 * 1024 * 1024

ROW_TILE = 512
ATT_TILE = 512
ATT_BLOCK = 128
DN_TILE = 512
CHUNK = 64
FF_CHUNK = 1024


def _dot(a, b):
    return jnp.dot(a, b, preferred_element_type=F32)


def _dot_nt(a, b):
    return lax.dot_general(a, b, (((1,), (1,)), ((), ())), preferred_element_type=F32)


def _dot_tn(a, b):
    return lax.dot_general(a, b, (((0,), (0,)), ((), ())), preferred_element_type=F32)


def _sigmoid(x):
    return 1.0 / (1.0 + jnp.exp(-x))


def _silu(x):
    return x * _sigmoid(x)


def _softplus(x):
    return jnp.maximum(x, 0.0) + jnp.log(1.0 + jnp.exp(-jnp.abs(x)))


def _layer_norm(x, g, b):
    mu = jnp.mean(x, axis=-1, keepdims=True)
    xc = x - mu
    var = jnp.mean(xc * xc, axis=-1, keepdims=True)
    return xc * lax.rsqrt(var + LN_EPS) * g + b


def _split3(x):
    hi = x.astype(BF16)
    r1 = x - hi.astype(F32)
    mid = r1.astype(BF16)
    lo = (r1 - mid.astype(F32)).astype(BF16)
    return hi, mid, lo


def _dot_exact_rhs(sel, x):
    hi, mid, lo = _split3(x)
    return _dot(sel, hi) + _dot(sel, mid) + _dot(sel, lo)


def _dot_exact_lhs(x, sel):
    hi, mid, lo = _split3(x)
    return _dot(hi, sel) + _dot(mid, sel) + _dot(lo, sel)


def _iota2(shape, axis):
    return lax.broadcasted_iota(jnp.int32, shape, axis)


def _mod_kernel(c_ref, w_ref, b_ref, o_ref):
    s = _silu(c_ref[...]).astype(BF16)
    o_ref[0] = _dot(s, w_ref[0].astype(BF16)) + b_ref[0]


def _modulation(c_all, w_mod, b_mod):
    bp = c_all.shape[0]
    nblk = w_mod.shape[2] // D_MODEL
    return pl.pallas_call(
        _mod_kernel,
        grid=(DEPTH, nblk),
        in_specs=[
            pl.BlockSpec((bp, D_MODEL), lambda l, k: (0, 0)),
            pl.BlockSpec((1, D_MODEL, D_MODEL), lambda l, k: (l, 0, k)),
            pl.BlockSpec((1, 1, D_MODEL), lambda l, k: (l, 0, k)),
        ],
        out_specs=pl.BlockSpec((1, bp, D_MODEL), lambda l, k: (l, 0, k)),
        out_shape=jax.ShapeDtypeStruct((DEPTH, bp, 6 * D_MODEL), F32),
        compiler_params=pltpu.CompilerParams(dimension_semantics=("arbitrary", "arbitrary"),
                                             vmem_limit_bytes=VMEM_LIMIT_BYTES),
        name="modulation",
    )(c_all, w_mod, b_mod.reshape(DEPTH, 1, 6 * D_MODEL))


def _inproj_kernel(apply_ln, x_ref, sh_ref, sc_ref, lng_ref, lnb_ref, w_ref, wg_ref, *out_refs):
    x = x_ref[0]
    if apply_ln:
        x = _layer_norm(x, lng_ref[...], lnb_ref[...])
        out_refs[6][0] = x
    h = (x * (1.0 + sc_ref[0]) + sh_ref[0]).astype(BF16)
    p = _dot(h, w_ref[...])
    o_conv, o_q, o_kv, o_dn, o_z, o_g = out_refs[:6]
    o_conv[0] = p[:, :OFF_ATT].astype(BF16)
    o_q[0] = (p[:, OFF_ATT:OFF_ATT + ATT_Q] * (HEAD_DIM ** -0.5)).astype(BF16)
    o_kv[0] = p[:, OFF_ATT + ATT_Q:OFF_DN].astype(BF16)
    o_dn[0] = p[:, OFF_DN:OFF_DN + 3 * DN_W].astype(BF16)
    o_z[0] = p[:, OFF_DN + 3 * DN_W:OFF_GATE].astype(BF16)
    o_g[0] = _dot(h, wg_ref[...])


def _in_projection(x, mod_l, b0, ln_g, ln_b, w_main, w_gate, apply_ln):
    bsz, s, _ = x.shape
    tm = min(ROW_TILE, s)
    row = lambda b, i: (b, i, 0)
    full2 = lambda b, i: (0, 0)
    widths = (OFF_ATT, ATT_Q, 2 * ATT_KV, 3 * DN_W, DN_W, LANES)
    dtypes = (BF16, BF16, BF16, BF16, BF16, F32)
    out_shape = [jax.ShapeDtypeStruct((bsz, s, w), dt) for w, dt in zip(widths, dtypes)]
    out_specs = [pl.BlockSpec((1, tm, w), row) for w in widths]
    if apply_ln:
        out_shape.append(jax.ShapeDtypeStruct((bsz, s, D_MODEL), F32))
        out_specs.append(pl.BlockSpec((1, tm, D_MODEL), row))
    return pl.pallas_call(
        functools.partial(_inproj_kernel, apply_ln),
        grid=(bsz, s // tm),
        in_specs=[
            pl.BlockSpec((1, tm, D_MODEL), row),
            pl.BlockSpec((1, 1, D_MODEL), lambda b, i: (b0 + b, 0, 0)),
            pl.BlockSpec((1, 1, D_MODEL), lambda b, i: (b0 + b, 0, 1)),
            pl.BlockSpec((1, D_MODEL), full2),
            pl.BlockSpec((1, D_MODEL), full2),
            pl.BlockSpec((D_MODEL, OFF_GATE), full2, pipeline_mode=pl.Buffered(1)),
            pl.BlockSpec((D_MODEL, LANES), full2, pipeline_mode=pl.Buffered(1)),
        ],
        out_specs=out_specs,
        out_shape=out_shape,
        compiler_params=pltpu.CompilerParams(dimension_semantics=("parallel", "parallel"),
                                             vmem_limit_bytes=VMEM_LIMIT_BYTES),
        name="in_projection",
    )(x, mod_l, mod_l, ln_g, ln_b, w_main, w_gate)


def _mix_ab_kernel(seq_len, pc_ref, pc_prev_ref, pc_next_ref, q_ref, kv_ref, kv_prev_ref, kv_next_ref,
                   cw_ref, sink_ref, o_ref, cu_ext, ka_ref, kb_ref, va_ref, vb_ref):
    i = pl.program_id(1)
    n = pl.num_programs(1)
    tq = q_ref.shape[1]
    has_prev = (i > 0).astype(F32)
    has_next = (i < n - 1).astype(F32)

    pc = pc_ref[0]
    bg = pc[:, :CONV_CH].astype(F32)
    cu = pc[:, CONV_CH:2 * CONV_CH].astype(F32) * pc[:, 2 * CONV_CH:].astype(F32)
    pp = pc_prev_ref[0].astype(F32)[BF16_SUBLANES - F32_SUBLANES:, :]
    pn = pc_next_ref[0].astype(F32)[:F32_SUBLANES, :]
    cu_ext[0:F32_SUBLANES, :] = (pp[:, CONV_CH:2 * CONV_CH] * pp[:, 2 * CONV_CH:]) * has_prev
    cu_ext[F32_SUBLANES:F32_SUBLANES + tq, :] = cu
    cu_ext[F32_SUBLANES + tq:, :] = (pn[:, CONV_CH:2 * CONV_CH] * pn[:, 2 * CONV_CH:]) * has_next
    cw = cw_ref[...]
    conv = (cw[0:1, :] * cu_ext[F32_SUBLANES - 1:F32_SUBLANES - 1 + tq, :] + cw[1:2, :] * cu
            + cw[2:3, :] * cu_ext[F32_SUBLANES + 1:F32_SUBLANES + 1 + tq, :])
    o_ref[0, :, :CONV_CH] = (bg * conv).astype(BF16)

    def fill(r0, blk):
        kk = blk[:, :LANES].astype(F32)
        vv = blk[:, LANES:].astype(F32)
        lo = _iota2(kk.shape, 1) < HEAD_DIM
        kks = pltpu.roll(kk, HEAD_DIM, 1)
        vvs = pltpu.roll(vv, HEAD_DIM, 1)
        rows = pl.ds(r0, blk.shape[0])
        ka_ref[0, rows, :] = jnp.where(lo, kk, 0.0).astype(BF16)
        kb_ref[0, rows, :] = jnp.where(lo, 0.0, kks).astype(BF16)
        ka_ref[1, rows, :] = jnp.where(lo, kks, 0.0).astype(BF16)
        kb_ref[1, rows, :] = jnp.where(lo, 0.0, kk).astype(BF16)
        va_ref[0, rows, :] = jnp.where(lo, vv, 0.0).astype(BF16)
        vb_ref[0, rows, :] = jnp.where(lo, 0.0, vvs).astype(BF16)
        va_ref[1, rows, :] = jnp.where(lo, vvs, 0.0).astype(BF16)
        vb_ref[1, rows, :] = jnp.where(lo, 0.0, vv).astype(BF16)

    fill(0, kv_prev_ref[0])
    fill(ATT_BLOCK, kv_ref[0])
    fill(ATT_BLOCK + tq, kv_next_ref[0])

    nwin = 3 * ATT_BLOCK
    qi = _iota2((ATT_BLOCK, nwin), 0)
    ki = _iota2((ATT_BLOCK, nwin), 1)
    rel = ki - ATT_BLOCK - qi
    dist = jnp.abs(rel).astype(F32)
    lane_lo = _iota2((ATT_BLOCK, LANES), 1) < HEAD_DIM
    grp = N_HEADS // N_KV_HEADS

    for j in range(tq // ATT_BLOCK):
        kpos = i * tq + (j - 1) * ATT_BLOCK + ki
        valid = (jnp.abs(rel) <= WINDOW) & (kpos >= 0) & (kpos < seq_len)
        win = pl.ds(j * ATT_BLOCK, nwin)
        qrows = pl.ds(j * ATT_BLOCK, ATT_BLOCK)
        for hk in range(N_KV_HEADS):
            kas, kbs = ka_ref[hk, win, :], kb_ref[hk, win, :]
            vas, vbs = va_ref[hk, win, :], vb_ref[hk, win, :]
            for pr in range(grp // 2):
                pair = hk * (grp // 2) + pr
                qp = q_ref[0, qrows, pair * LANES:(pair + 1) * LANES]
                acc = None
                rcp = []
                for par, kz, vz in ((0, kas, vas), (1, kbs, vbs)):
                    head = 2 * pair + par
                    slope = 2.0 ** (-8.0 * (head + 1) / N_HEADS)
                    sc = _dot_nt(qp, kz) - slope * dist
                    sc = jnp.where(valid, sc, -jnp.inf)
                    sink = sink_ref[head]
                    m = jnp.maximum(jnp.max(sc, axis=-1, keepdims=True), sink)
                    e = jnp.exp(sc - m)
                    denom = jnp.sum(e, axis=-1, keepdims=True) + jnp.exp(sink - m)
                    rcp.append(1.0 / denom)
                    pv = _dot(e.astype(BF16), vz)
                    acc = pv if acc is None else acc + pv
                out = acc * jnp.where(lane_lo, rcp[0], rcp[1])
                o_ref[0, qrows, CONV_CH + pair * LANES:CONV_CH + (pair + 1) * LANES] = out.astype(BF16)


def _mix_ab(pconv, q, kv, conv_w, sink):
    bsz, s, _ = q.shape
    tq = min(ATT_TILE, s)
    nh = tq // BF16_SUBLANES
    nb = tq // ATT_BLOCK
    row = lambda b, i: (b, i, 0)
    return pl.pallas_call(
        functools.partial(_mix_ab_kernel, s),
        grid=(bsz, s // tq),
        in_specs=[
            pl.BlockSpec((1, tq, OFF_ATT), row),
            pl.BlockSpec((1, BF16_SUBLANES, OFF_ATT), lambda b, i: (b, jnp.maximum(i * nh - 1, 0), 0)),
            pl.BlockSpec((1, BF16_SUBLANES, OFF_ATT),
                         lambda b, i: (b, jnp.minimum((i + 1) * nh, s // BF16_SUBLANES - 1), 0)),
            pl.BlockSpec((1, tq, ATT_Q), row),
            pl.BlockSpec((1, tq, 2 * ATT_KV), row),
            pl.BlockSpec((1, ATT_BLOCK, 2 * ATT_KV), lambda b, i: (b, jnp.maximum(i * nb - 1, 0), 0)),
            pl.BlockSpec((1, ATT_BLOCK, 2 * ATT_KV),
                         lambda b, i: (b, jnp.minimum((i + 1) * nb, s // ATT_BLOCK - 1), 0)),
            pl.BlockSpec((F32_SUBLANES, CONV_CH), lambda b, i: (0, 0)),
            pl.BlockSpec(memory_space=pltpu.SMEM),
        ],
        out_specs=pl.BlockSpec((1, tq, CONV_CH + ATT_Q), row),
        out_shape=jax.ShapeDtypeStruct((bsz, s, CONV_CH + ATT_Q), BF16),
        scratch_shapes=[
            pltpu.VMEM((tq + 2 * F32_SUBLANES, CONV_CH), F32),
            pltpu.VMEM((N_KV_HEADS, tq + 2 * ATT_BLOCK, LANES), BF16),
            pltpu.VMEM((N_KV_HEADS, tq + 2 * ATT_BLOCK, LANES), BF16),
            pltpu.VMEM((N_KV_HEADS, tq + 2 * ATT_BLOCK, LANES), BF16),
            pltpu.VMEM((N_KV_HEADS, tq + 2 * ATT_BLOCK, LANES), BF16),
        ],
        compiler_params=pltpu.CompilerParams(dimension_semantics=("parallel", "parallel"),
                                             vmem_limit_bytes=VMEM_LIMIT_BYTES),
        name="conv_window_attention",
    )(pconv, pconv, pconv, q, kv, kv, kv, conv_w, sink)


N_LEVELS = 6
CM_EYE, CM_NEG, CM_LEVEL0 = 0, 1, 2


def _dn_constants(reverse, ts):
    i = np.arange(CHUNK)[:, None]
    j = (np.arange(DN_W) % DN_DIM)[None, :]
    tri_incl = (i <= j) if reverse else (i >= j)
    rows = [(i == j).astype(np.float32), np.where(tri_incl, 0.0, -np.inf).astype(np.float32)]
    for lvl in range(1, N_LEVELS + 1):
        half = 1 << (lvl - 1)
        same_blk = (i >> lvl) == (j >> lvl)
        i_hi, j_hi = (i & half) != 0, (j & half) != 0
        rows.append((same_blk & ~i_hi & j_hi if reverse else same_blk & i_hi & ~j_hi).astype(np.float32))
    chunk_masks = np.stack(rows)
    r = np.arange(DN_W)
    mask_bd = (r[:, None] // DN_DIM) == (r[None, :] // DN_DIM)
    bd_f32 = np.stack([mask_bd.astype(np.float32), np.eye(DN_W, dtype=np.float32)])
    t = np.arange(ts)
    same_chunk = (t[:, None] // CHUNK) == (t[None, :] // CHUNK)
    cum = same_chunk & ((t[None, :] >= t[:, None]) if reverse else (t[None, :] <= t[:, None]))
    goff, boff = (DN_HEADS, 3 * DN_HEADS) if reverse else (0, 2 * DN_HEADS)
    er = np.arange(LANES)[:, None]
    ec = (np.arange(DN_W) // DN_DIM)[None, :]
    expand = np.stack([er == ec + goff, er == ec + boff])
    return (jnp.asarray(chunk_masks), jnp.asarray(bd_f32), jnp.asarray(mask_bd, BF16),
            jnp.asarray(cum, BF16), jnp.asarray(expand, BF16))


def _dn_tile(reverse, q, k, v, gcol, bcol, st, cm_ref, bdf_ref, bdb_ref):
    nchunk = q.shape[0] // CHUNK
    cs = range(nchunk)
    rows = [slice(c * CHUNK, (c + 1) * CHUNK) for c in cs]

    def bd(y):
        return jnp.concatenate([y] * DN_HEADS, axis=0) * bdb_ref[...]

    eye_p = cm_ref[CM_EYE]
    egc = jnp.exp(gcol)
    kbg = k * (bcol * egc)
    vb = v * bcol
    qe = q * egc
    gcs = [gcol[r] for r in rows]
    grow = [jnp.sum(eye_p * g, axis=0, keepdims=True) for g in gcs]
    gtot = [g[0:1, :] if reverse else g[CHUNK - 1:CHUNK, :] for g in gcs]
    decay = [jnp.exp(g - gr + cm_ref[CM_NEG]) for g, gr in zip(gcs, grow)]
    kb16 = [k[r].astype(BF16) for r in rows]
    gq = [_dot_nt(jnp.concatenate([kb, q[r].astype(BF16)], axis=0), bd(kb)) for kb, r in zip(kb16, rows)]
    lb = [g[:CHUNK] * bcol[r] * d for g, r, d in zip(gq, rows, decay)]
    ab = [(g[CHUNK:] * d).astype(BF16) for g, d in zip(gq, decay)]
    x = [eye_p - cm_ref[CM_LEVEL0] * l for l in lb]
    for lvl in range(1, N_LEVELS):
        xb = [xc.astype(BF16) for xc in x]
        z = [_dot((cm_ref[CM_LEVEL0 + lvl] * l).astype(BF16), bd(b)) for l, b in zip(lb, xb)]
        x = [xc - _dot(b, bd(zc.astype(BF16))) for xc, b, zc in zip(x, xb, z)]
    xb = [xc.astype(BF16) for xc in x]
    u = [_dot(b, bd(vb[r].astype(BF16))) for b, r in zip(xb, rows)]
    w = [_dot(b, bd(kbg[r].astype(BF16))) for b, r in zip(xb, rows)]
    p = [qe[r] - _dot(a, bd(wc.astype(BF16))) for r, a, wc in zip(rows, ab, w)]
    rr = [_dot(a, bd(uc.astype(BF16))) for a, uc in zip(ab, u)]
    kd = [(k[r] * jnp.exp(gt - g)).astype(BF16) for r, gt, g in zip(rows, gtot, gcs)]
    full = [_dot_tn(jnp.concatenate([wc, uc], axis=1).astype(BF16), kdc) for wc, uc, kdc in zip(w, u, kd)]
    bdm = [(bdf_ref[1] * jnp.exp(gt) - bdf_ref[0] * f[:DN_W]).astype(BF16) for gt, f in zip(gtot, full)]
    nfull = [bdf_ref[0] * f[DN_W:] for f in full]
    nt = [nf[0:64] + nf[64:128] + nf[128:192] + nf[192:256] for nf in nfull]
    outs = [None] * nchunk
    for c in (reversed(cs) if reverse else cs):
        sb = st.astype(BF16)
        outs[c] = _dot_nt(p[c].astype(BF16), bd(sb)) + rr[c]
        st = _dot(sb, bdm[c]) + nt[c]
    return outs, st


def _dn_kernel(reverse, x_ref, x_prev_ref, x_next_ref, g_ref, cw_ref, gp_ref, cm_ref, bdf_ref, bdb_ref,
               cum_ref, exp_ref, *rest):
    if reverse:
        z_ref, of_ref, ng_ref, o_ref, ext, st_ref = rest
    else:
        o_ref, ext, st_ref = rest
    i = pl.program_id(1)
    n = pl.num_programs(1)
    t = (n - 1 - i) if reverse else i
    ts = x_ref.shape[1]
    has_prev = (t > 0).astype(F32)
    has_next = (t < n - 1).astype(F32)

    @pl.when(i == 0)
    def _():
        st_ref[...] = jnp.zeros(st_ref.shape, F32)

    ext[0:F32_SUBLANES, :] = x_prev_ref[0].astype(F32)[BF16_SUBLANES - F32_SUBLANES:, :] * has_prev
    ext[F32_SUBLANES:F32_SUBLANES + ts, :] = x_ref[0].astype(F32)
    ext[F32_SUBLANES + ts:, :] = x_next_ref[0].astype(F32)[:F32_SUBLANES, :] * has_next
    cw = cw_ref[...]
    qkv = _silu(cw[0:1, :] * ext[F32_SUBLANES - 1:F32_SUBLANES - 1 + ts, :]
                + cw[1:2, :] * ext[F32_SUBLANES:F32_SUBLANES + ts, :]
                + cw[2:3, :] * ext[F32_SUBLANES + 1:F32_SUBLANES + 1 + ts, :])
    def head_sumsq(y):
        y2 = y * y
        hi = y2.astype(BF16)
        lo = (y2 - hi.astype(F32)).astype(BF16)
        return _dot(hi, bdb_ref[...]) + _dot(lo, bdb_ref[...])

    q = qkv[:, :DN_W]
    k = qkv[:, DN_W:2 * DN_W]
    v = qkv[:, 2 * DN_W:]
    q = q * (lax.rsqrt(head_sumsq(q) + RMS_EPS) * (DN_DIM ** -0.5))
    k = k * lax.rsqrt(head_sumsq(k) + RMS_EPS)

    gates = g_ref[0]
    gp = gp_ref[...]
    glog = -gp[0:1, :] * _softplus(gates + gp[1:2, :])
    beta = _sigmoid(gates)
    gcum = _dot_exact_rhs(cum_ref[...], glog)
    gcol = _dot_exact_lhs(gcum, exp_ref[0])
    bcol = _dot_exact_lhs(beta, exp_ref[1])

    outs, st = _dn_tile(reverse, q, k, v, gcol, bcol, st_ref[...], cm_ref, bdf_ref, bdb_ref)
    st_ref[...] = st
    o = jnp.concatenate(outs, axis=0)
    if reverse:
        o = o + of_ref[0]
        ms = head_sumsq(o) * (1.0 / DN_DIM)
        y = o * lax.rsqrt(ms + RMS_EPS) * ng_ref[...] * _silu(z_ref[0].astype(F32))
        o_ref[0] = y.astype(BF16)
    else:
        o_ref[0] = o


def _delta_pass(reverse, dnqkv, gates, conv_w, gate_params, z=None, o_fwd=None, norm_g=None):
    bsz, s, _ = dnqkv.shape
    ts = min(DN_TILE, s)
    n = s // ts
    nh = ts // BF16_SUBLANES
    pos = (lambda i: n - 1 - i) if reverse else (lambda i: i)
    row = lambda b, i: (b, pos(i), 0)
    full2 = lambda b, i: (0, 0)
    in_specs = [
        pl.BlockSpec((1, ts, 3 * DN_W), row),
        pl.BlockSpec((1, BF16_SUBLANES, 3 * DN_W), lambda b, i: (b, jnp.maximum(pos(i) * nh - 1, 0), 0)),
        pl.BlockSpec((1, BF16_SUBLANES, 3 * DN_W),
                     lambda b, i: (b, jnp.minimum((pos(i) + 1) * nh, s // BF16_SUBLANES - 1), 0)),
        pl.BlockSpec((1, ts, LANES), row),
        pl.BlockSpec((F32_SUBLANES, 3 * DN_W), full2),
        pl.BlockSpec((F32_SUBLANES, LANES), full2),
    ]
    consts = _dn_constants(reverse, ts)
    in_specs += [pl.BlockSpec(c.shape, (lambda nd: lambda b, i: (0,) * nd)(c.ndim), pipeline_mode=pl.Buffered(1))
                 for c in consts]
    args = [dnqkv, dnqkv, dnqkv, gates, conv_w, gate_params, *consts]
    if reverse:
        in_specs += [pl.BlockSpec((1, ts, DN_W), row), pl.BlockSpec((1, ts, DN_W), row),
                     pl.BlockSpec((1, DN_W), full2)]
        args += [z, o_fwd, norm_g]
        out_dtype = BF16
    else:
        out_dtype = F32
    return pl.pallas_call(
        functools.partial(_dn_kernel, reverse),
        grid=(bsz, n),
        in_specs=in_specs,
        out_specs=pl.BlockSpec((1, ts, DN_W), row),
        out_shape=jax.ShapeDtypeStruct((bsz, s, DN_W), out_dtype),
        scratch_shapes=[pltpu.VMEM((ts + 2 * F32_SUBLANES, 3 * DN_W), F32),
                        pltpu.VMEM((DN_DIM, DN_W), F32)],
        compiler_params=pltpu.CompilerParams(dimension_semantics=("parallel", "arbitrary"),
                                             vmem_limit_bytes=VMEM_LIMIT_BYTES),
        name="delta_bwd" if reverse else "delta_fwd",
    )(*args)


def _out_mlp_kernel(x_ref, yab_ref, yc_ref, g1_ref, sh2_ref, sc2_ref, g2_ref, wo_ab_ref, wo_c_ref,
                    ln1g_ref, ln1b_ref, w1_ref, b1_ref, w2_ref, b2_ref, ln2g_ref, ln2b_ref, o_ref):
    x = x_ref[0]
    y = _dot(yab_ref[0], wo_ab_ref[...]) + _dot(yc_ref[0], wo_c_ref[...])
    x1 = _layer_norm(ALPHA * x + (1.0 + g1_ref[0]) * y, ln1g_ref[...], ln1b_ref[...])
    h = (x1 * (1.0 + sc2_ref[0]) + sh2_ref[0]).astype(BF16)
    f = None
    for c in range(D_FF // FF_CHUNK):
        cols = slice(c * FF_CHUNK, (c + 1) * FF_CHUNK)
        a = jnp.maximum(_dot(h, w1_ref[:, cols]) + b1_ref[:, cols], 0.0)
        part = _dot((a * a).astype(BF16), w2_ref[cols, :])
        f = part if f is None else f + part
    f = f + b2_ref[...]
    o_ref[0] = _layer_norm(ALPHA * x1 + (1.0 + g2_ref[0]) * f, ln2g_ref[...], ln2b_ref[...])


def _out_mlp(x, yab, yc, mod_l, b0, wo_ab, wo_c, ln1g, ln1b, w1, b1, w2, b2, ln2g, ln2b):
    bsz, s, _ = x.shape
    tm = min(ROW_TILE, s)
    row = lambda b, i: (b, i, 0)
    full2 = lambda b, i: (0, 0)
    modspec = lambda k: pl.BlockSpec((1, 1, D_MODEL), lambda b, i: (b0 + b, 0, k))
    const = lambda shape: pl.BlockSpec(shape, full2, pipeline_mode=pl.Buffered(1))
    return pl.pallas_call(
        _out_mlp_kernel,
        grid=(bsz, s // tm),
        in_specs=[
            pl.BlockSpec((1, tm, D_MODEL), row),
            pl.BlockSpec((1, tm, CONV_CH + ATT_Q), row),
            pl.BlockSpec((1, tm, DN_W), row),
            modspec(2), modspec(3), modspec(4), modspec(5),
            const((CONV_CH + ATT_Q, D_MODEL)), const((DN_W, D_MODEL)),
            const((1, D_MODEL)), const((1, D_MODEL)),
            const((D_MODEL, D_FF)), const((1, D_FF)),
            const((D_FF, D_MODEL)), const((1, D_MODEL)),
            const((1, D_MODEL)), const((1, D_MODEL)),
        ],
        out_specs=pl.BlockSpec((1, tm, D_MODEL), row),
        out_shape=jax.ShapeDtypeStruct((bsz, s, D_MODEL), F32),
        compiler_params=pltpu.CompilerParams(dimension_semantics=("parallel", "parallel"),
                                             vmem_limit_bytes=VMEM_LIMIT_BYTES),
        name="out_mlp",
    )(x, yab, yc, mod_l, mod_l, mod_l, mod_l, wo_ab, wo_c, ln1g, ln1b, w1, b1, w2, b2, ln2g, ln2b)


def _pad_rows(a, rows):
    return jnp.concatenate([a, jnp.zeros((rows - a.shape[0],) + a.shape[1:], a.dtype)], axis=0)


def _trunk(x, mod, b0, prm):
    for l in range(DEPTH):
        mod_l = mod[l]
        outs = _in_projection(x, mod_l, b0, prm["ln_in_g"], prm["ln_in_b"], prm["w_main"][l], prm["w_gate"][l],
                              apply_ln=(l == 0))
        pconv, q, kv, dnqkv, z, gates = outs[:6]
        if l == 0:
            x = outs[6]
        yab = _mix_ab(pconv, q, kv, prm["conv_a_w"][l], prm["attn_sink"][l])
        o_fwd = _delta_pass(False, dnqkv, gates, prm["dn_conv_w"][l], prm["gate_params"][l])
        yc = _delta_pass(True, dnqkv, gates, prm["dn_conv_w"][l], prm["gate_params"][l],
                         z=z, o_fwd=o_fwd, norm_g=prm["dn_norm_g"][l])
        x = _out_mlp(x, yab, yc, mod_l, b0, prm["wo_ab"][l], prm["wo_c"][l], prm["ln1_g"][l], prm["ln1_b"][l],
                     prm["w1"][l], prm["b1"][l], prm["w2"][l], prm["b2"][l], prm["ln2_g"][l], prm["ln2_b"][l])
    return x


def kernel(x_prompt, x_sample, c_prompt, c_sample, ln_in_g, ln_in_b, w_mod, b_mod, w_in, conv_a_w, attn_sink,
           dn_conv_w, dn_a_log_f, dn_a_log_b, dn_dt_bias_f, dn_dt_bias_b, dn_norm_g, w_out, ln1_g, ln1_b,
           w1, b1, w2, b2, ln2_g, ln2_b):
    nb_p, nb_s = c_prompt.shape[0], c_sample.shape[0]
    bp = -(-(nb_p + nb_s) // BF16_SUBLANES) * BF16_SUBLANES
    c_all = _pad_rows(jnp.concatenate([c_prompt, c_sample], axis=0), bp)
    mod = _modulation(c_all, w_mod, b_mod)
    mod = mod.reshape(DEPTH, bp, 1, 6 * D_MODEL)

    row = lambda a: a.reshape(DEPTH, 1, -1)
    zeros4 = jnp.zeros((DEPTH, DN_HEADS), F32)
    gp0 = jnp.concatenate([jnp.exp(dn_a_log_f), jnp.exp(dn_a_log_b), zeros4, zeros4], axis=1)
    gp1 = jnp.concatenate([dn_dt_bias_f, dn_dt_bias_b, zeros4, zeros4], axis=1)
    gate_params = jnp.stack([gp0, gp1], axis=1)
    gate_params = jnp.pad(gate_params, ((0, 0), (0, F32_SUBLANES - 2), (0, LANES - 4 * DN_HEADS)))
    pad_taps = lambda w: jnp.pad(w, ((0, 0), (0, F32_SUBLANES - w.shape[1]), (0, 0)))
    prm = dict(
        ln_in_g=ln_in_g.reshape(1, -1), ln_in_b=ln_in_b.reshape(1, -1),
        w_main=w_in[:, :, :OFF_GATE].astype(BF16),
        w_gate=jnp.pad(w_in[:, :, OFF_GATE:], ((0, 0), (0, 0), (0, LANES - 4 * DN_HEADS))).astype(BF16),
        conv_a_w=pad_taps(conv_a_w), attn_sink=attn_sink,
        dn_conv_w=pad_taps(dn_conv_w), gate_params=gate_params,
        dn_norm_g=jnp.tile(dn_norm_g, (1, DN_HEADS)).reshape(DEPTH, 1, DN_W),
        wo_ab=w_out[:, :CONV_CH + ATT_Q, :].astype(BF16), wo_c=w_out[:, CONV_CH + ATT_Q:, :].astype(BF16),
        ln1_g=row(ln1_g), ln1_b=row(ln1_b), w1=w1.astype(BF16), b1=row(b1), w2=w2.astype(BF16), b2=row(b2),
        ln2_g=row(ln2_g), ln2_b=row(ln2_b),
    )
    y_prompt = _trunk(x_prompt, mod, 0, prm)
    y_sample = _trunk(x_sample, mod, nb_p, prm)
    return (y_prompt, y_sample)
```

```python
import functools

import numpy as np
import jax
import jax.numpy as jnp
from jax import lax
from jax.experimental import pallas as pl
from jax.experimental.pallas import tpu as pltpu

F32 = jnp.float32
BF16 = jnp.bfloat16

D_MODEL = 1024
DEPTH = 4
CONV_CH = 256
N_HEADS = 8
N_KV_HEADS = 2
HEAD_DIM = 64
WINDOW = 128
DN_HEADS = 4
DN_DIM = 64
DN_W = DN_HEADS * DN_DIM
D_FF = 4 * D_MODEL
ATT_Q = N_HEADS * HEAD_DIM
ATT_KV = N_KV_HEADS * HEAD_DIM
OFF_ATT = 3 * CONV_CH
OFF_DN = OFF_ATT + ATT_Q + 2 * ATT_KV
OFF_GATE = OFF_DN + 4 * DN_W
D_IN = OFF_GATE + 4 * DN_HEADS
ALPHA = (2.0 * DEPTH) ** 0.25
LN_EPS = 1e-5
RMS_EPS = 1e-6
LOG2E = 1.4426950408889634

LANES = 128
BF16_SUBLANES = 16
F32_SUBLANES = 8
VMEM_LIMIT_BYTES = 56 * 1024 * 1024

ROW_TILE = 512
ATT_TILE = 512
ATT_BLOCK = 128
DN_TILE = 512
CHUNK = 64
FF_CHUNK = 1024


def _dot(a, b):
    return jnp.dot(a, b, preferred_element_type=F32)


def _dot_nt(a, b):
    return lax.dot_general(a, b, (((1,), (1,)), ((), ())), preferred_element_type=F32)


def _dot_tn(a, b):
    return lax.dot_general(a, b, (((0,), (0,)), ((), ())), preferred_element_type=F32)


def _sigmoid(x):
    return 1.0 / (1.0 + jnp.exp(-x))


def _silu(x):
    return x * _sigmoid(x)


def _softplus(x):
    return jnp.maximum(x, 0.0) + jnp.log(1.0 + jnp.exp(-jnp.abs(x)))


def _layer_norm(x, g, b):
    mu = jnp.mean(x, axis=-1, keepdims=True)
    xc = x - mu
    var = jnp.mean(xc * xc, axis=-1, keepdims=True)
    return xc * lax.rsqrt(var + LN_EPS) * g + b


def _split3(x):
    hi = x.astype(BF16)
    r1 = x - hi.astype(F32)
    mid = r1.astype(BF16)
    lo = (r1 - mid.astype(F32)).astype(BF16)
    return hi, mid, lo


def _dot_exact_rhs(sel, x):
    hi, mid, lo = _split3(x)
    return _dot(sel, hi) + _dot(sel, mid) + _dot(sel, lo)


def _dot_exact_lhs(x, sel):
    hi, mid, lo = _split3(x)
    return _dot(hi, sel) + _dot(mid, sel) + _dot(lo, sel)


def _iota2(shape, axis):
    return lax.broadcasted_iota(jnp.int32, shape, axis)


def _mod_kernel(c_ref, w_ref, b_ref, o_ref):
    s = _silu(c_ref[...]).astype(BF16)
    o_ref[0] = _dot(s, w_ref[0].astype(BF16)) + b_ref[0]


def _modulation(c_all, w_mod, b_mod):
    bp = c_all.shape[0]
    nblk = w_mod.shape[2] // D_MODEL
    return pl.pallas_call(
        _mod_kernel,
        grid=(DEPTH, nblk),
        in_specs=[
            pl.BlockSpec((bp, D_MODEL), lambda l, k: (0, 0)),
            pl.BlockSpec((1, D_MODEL, D_MODEL), lambda l, k: (l, 0, k)),
            pl.BlockSpec((1, 1, D_MODEL), lambda l, k: (l, 0, k)),
        ],
        out_specs=pl.BlockSpec((1, bp, D_MODEL), lambda l, k: (l, 0, k)),
        out_shape=jax.ShapeDtypeStruct((DEPTH, bp, 6 * D_MODEL), F32),
        compiler_params=pltpu.CompilerParams(dimension_semantics=("arbitrary", "arbitrary"),
                                             vmem_limit_bytes=VMEM_LIMIT_BYTES),
        name="modulation",
    )(c_all, w_mod, b_mod.reshape(DEPTH, 1, 6 * D_MODEL))


def _conv3(ext, mid, taps, rows):
    return (taps[1:2, :] * mid + taps[0:1, :] * ext[F32_SUBLANES - 1:F32_SUBLANES - 1 + rows, :]
            + taps[2:3, :] * ext[F32_SUBLANES + 1:F32_SUBLANES + 1 + rows, :])


def _head_sumsq(y, ones_bd):
    y2 = y * y
    hi = y2.astype(BF16)
    lo = (y2 - hi.astype(F32)).astype(BF16)
    return _dot(hi, ones_bd) + _dot(lo, ones_bd)


def _inproj_kernel(apply_ln, x_ref, xp_ref, xn_ref, sh_ref, sc_ref, lng_ref, lnb_ref, w_ref, wg_ref,
                   cwa_ref, cwd_ref, bdb_ref, *rest):
    n_out = 7 if apply_ln else 6
    out_refs, (cu_ext, dn_ext) = rest[:n_out], rest[n_out:]
    o_ya, o_q, o_kv, o_dn, o_z, o_g = out_refs[:6]
    i = pl.program_id(1)
    n = pl.num_programs(1)
    tm = x_ref.shape[1]
    has_prev = (i > 0).astype(F32)
    has_next = (i < n - 1).astype(F32)
    x = x_ref[0]
    xh = jnp.concatenate([xp_ref[0], xn_ref[0]], axis=0)
    if apply_ln:
        x = _layer_norm(x, lng_ref[...], lnb_ref[...])
        xh = _layer_norm(xh, lng_ref[...], lnb_ref[...])
        out_refs[6][0] = x
    h = (x * (1.0 + sc_ref[0]) + sh_ref[0]).astype(BF16)
    hh = (xh * (1.0 + sc_ref[0]) + sh_ref[0]).astype(BF16)
    ph_d = _dot(hh, w_ref[:, OFF_DN:OFF_DN + 3 * DN_W])
    ph_a = _dot(hh, w_ref[:, CONV_CH:OFF_ATT])
    pd = _dot(h, w_ref[:, OFF_DN:OFF_DN + 3 * DN_W])
    pa = _dot(h, w_ref[:, :OFF_ATT])
    o_q[0] = (_dot(h, w_ref[:, OFF_ATT:OFF_ATT + ATT_Q]) * (HEAD_DIM ** -0.5 * LOG2E)).astype(BF16)
    o_kv[0] = _dot(h, w_ref[:, OFF_ATT + ATT_Q:OFF_DN]).astype(BF16)
    o_z[0] = _dot(h, w_ref[:, OFF_DN + 3 * DN_W:OFF_GATE]).astype(BF16)
    o_g[0] = _dot(h, wg_ref[...])

    cu = pa[:, CONV_CH:2 * CONV_CH] * pa[:, 2 * CONV_CH:]
    cuh = ph_a[:, :CONV_CH] * ph_a[:, CONV_CH:]
    cu_ext[0:F32_SUBLANES, :] = cuh[:F32_SUBLANES] * has_prev
    cu_ext[F32_SUBLANES:F32_SUBLANES + tm, :] = cu
    cu_ext[F32_SUBLANES + tm:, :] = cuh[F32_SUBLANES:] * has_next
    o_ya[0] = (pa[:, :CONV_CH] * _conv3(cu_ext, cu, cwa_ref[...], tm)).astype(BF16)

    dn_ext[0:F32_SUBLANES, :] = ph_d[:F32_SUBLANES] * has_prev
    dn_ext[F32_SUBLANES:F32_SUBLANES + tm, :] = pd
    dn_ext[F32_SUBLANES + tm:, :] = ph_d[F32_SUBLANES:] * has_next
    qkv = _silu(_conv3(dn_ext, pd, cwd_ref[...], tm))
    q = qkv[:, :DN_W]
    k = qkv[:, DN_W:2 * DN_W]
    ones_bd = bdb_ref[...]
    o_dn[0, :, :DN_W] = (q * (lax.rsqrt(_head_sumsq(q, ones_bd) + RMS_EPS) * (DN_DIM ** -0.5))).astype(BF16)
    o_dn[0, :, DN_W:2 * DN_W] = (k * lax.rsqrt(_head_sumsq(k, ones_bd) + RMS_EPS)).astype(BF16)
    o_dn[0, :, 2 * DN_W:] = qkv[:, 2 * DN_W:].astype(BF16)


def _block_diag_ones():
    r = np.arange(DN_W)
    return jnp.asarray((r[:, None] // DN_DIM) == (r[None, :] // DN_DIM), BF16)


def _in_projection(x, mod_l, b0, ln_g, ln_b, w_main, w_gate, conv_a_w, dn_conv_w, apply_ln):
    bsz, s, _ = x.shape
    tm = min(ROW_TILE, s)
    nh = tm // F32_SUBLANES
    row = lambda b, i: (b, i, 0)
    full2 = lambda b, i: (0, 0)
    const = lambda shape: pl.BlockSpec(shape, full2, pipeline_mode=pl.Buffered(1))
    widths = (CONV_CH, ATT_Q, 2 * ATT_KV, 3 * DN_W, DN_W, LANES)
    dtypes = (BF16, BF16, BF16, BF16, BF16, F32)
    out_shape = [jax.ShapeDtypeStruct((bsz, s, w), dt) for w, dt in zip(widths, dtypes)]
    out_specs = [pl.BlockSpec((1, tm, w), row) for w in widths]
    if apply_ln:
        out_shape.append(jax.ShapeDtypeStruct((bsz, s, D_MODEL), F32))
        out_specs.append(pl.BlockSpec((1, tm, D_MODEL), row))
    return pl.pallas_call(
        functools.partial(_inproj_kernel, apply_ln),
        grid=(bsz, s // tm),
        in_specs=[
            pl.BlockSpec((1, tm, D_MODEL), row),
            pl.BlockSpec((1, F32_SUBLANES, D_MODEL), lambda b, i: (b, jnp.maximum(i * nh - 1, 0), 0)),
            pl.BlockSpec((1, F32_SUBLANES, D_MODEL),
                         lambda b, i: (b, jnp.minimum((i + 1) * nh, s // F32_SUBLANES - 1), 0)),
            pl.BlockSpec((1, 1, D_MODEL), lambda b, i: (b0 + b, 0, 0)),
            pl.BlockSpec((1, 1, D_MODEL), lambda b, i: (b0 + b, 0, 1)),
            pl.BlockSpec((1, D_MODEL), full2),
            pl.BlockSpec((1, D_MODEL), full2),
            const((D_MODEL, OFF_GATE)),
            const((D_MODEL, LANES)),
            const((F32_SUBLANES, CONV_CH)),
            const((F32_SUBLANES, 3 * DN_W)),
            const((DN_W, DN_W)),
        ],
        out_specs=out_specs,
        out_shape=out_shape,
        scratch_shapes=[pltpu.VMEM((tm + 2 * F32_SUBLANES, CONV_CH), F32),
                        pltpu.VMEM((tm + 2 * F32_SUBLANES, 3 * DN_W), F32)],
        compiler_params=pltpu.CompilerParams(dimension_semantics=("parallel", "parallel"),
                                             vmem_limit_bytes=VMEM_LIMIT_BYTES),
        name="in_projection",
    )(x, x, x, mod_l, mod_l, ln_g, ln_b, w_main, w_gate, conv_a_w, dn_conv_w, _block_diag_ones())


def _attention_bias():
    qi = jnp.arange(ATT_BLOCK)[:, None]
    ki = jnp.arange(3 * ATT_BLOCK)[None, :]
    dist = jnp.abs(ki - ATT_BLOCK - qi)
    slopes = jnp.exp2(-8.0 * jnp.arange(1, N_HEADS + 1, dtype=F32) / N_HEADS)
    base = -(slopes * LOG2E)[:, None, None] * dist.astype(F32)[None]
    variants = []
    for var in range(4):
        valid = dist <= WINDOW
        if var & 1:
            valid = valid & (ki >= ATT_BLOCK)
        if var & 2:
            valid = valid & (ki < 2 * ATT_BLOCK)
        variants.append(jnp.where(valid[None], base, -jnp.inf))
    return jnp.stack(variants)


def _attention_kernel(q_ref, kv_ref, kv_prev_ref, kv_next_ref, bias_ref, sink_ref, o_ref,
                      ka_ref, kb_ref, va_ref, vb_ref):
    i = pl.program_id(1)
    n = pl.num_programs(1)
    tq = q_ref.shape[1]

    def fill(r0, blk):
        kk = blk[:, :LANES].astype(F32)
        vv = blk[:, LANES:].astype(F32)
        lo = _iota2(kk.shape, 1) < HEAD_DIM
        kks = pltpu.roll(kk, HEAD_DIM, 1)
        vvs = pltpu.roll(vv, HEAD_DIM, 1)
        rows = pl.ds(r0, blk.shape[0])
        ka_ref[0, rows, :] = jnp.where(lo, kk, 0.0).astype(BF16)
        kb_ref[0, rows, :] = jnp.where(lo, 0.0, kks).astype(BF16)
        ka_ref[1, rows, :] = jnp.where(lo, kks, 0.0).astype(BF16)
        kb_ref[1, rows, :] = jnp.where(lo, 0.0, kk).astype(BF16)
        va_ref[0, rows, :] = jnp.where(lo, vv, 0.0).astype(BF16)
        vb_ref[0, rows, :] = jnp.where(lo, 0.0, vvs).astype(BF16)
        va_ref[1, rows, :] = jnp.where(lo, vvs, 0.0).astype(BF16)
        vb_ref[1, rows, :] = jnp.where(lo, 0.0, vv).astype(BF16)

    fill(0, kv_prev_ref[0])
    fill(ATT_BLOCK, kv_ref[0])
    fill(ATT_BLOCK + tq, kv_next_ref[0])

    nwin = 3 * ATT_BLOCK
    nsub = tq // ATT_BLOCK
    lane_lo = _iota2((ATT_BLOCK, LANES), 1) < HEAD_DIM
    grp = N_HEADS // N_KV_HEADS

    heads = range(N_HEADS)
    for j in range(nsub):
        blk = i * nsub + j
        var = jnp.where(blk == 0, 1, 0) + jnp.where(blk == n * nsub - 1, 2, 0)
        win = pl.ds(j * ATT_BLOCK, nwin)
        qrows = pl.ds(j * ATT_BLOCK, ATT_BLOCK)
        kz = [(kb_ref if h % 2 else ka_ref)[h // grp, win, :] for h in heads]
        vz = [(vb_ref if h % 2 else va_ref)[h // grp, win, :] for h in heads]
        qp = [q_ref[0, qrows, (h // 2) * LANES:(h // 2 + 1) * LANES] for h in heads]
        sc = [_dot_nt(qp[h], kz[h]) + bias_ref[var, h] for h in heads]
        m = [jnp.maximum(jnp.max(sc[h], axis=-1, keepdims=True), sink_ref[h]) for h in heads]
        e = [jnp.exp2(sc[h] - m[h]) for h in heads]
        rcp = [1.0 / (jnp.sum(e[h], axis=-1, keepdims=True) + jnp.exp2(sink_ref[h] - m[h])) for h in heads]
        pv = [_dot(e[h].astype(BF16), vz[h]) for h in heads]
        for pair in range(N_HEADS // 2):
            out = (pv[2 * pair] + pv[2 * pair + 1]) * jnp.where(lane_lo, rcp[2 * pair], rcp[2 * pair + 1])
            o_ref[0, qrows, pair * LANES:(pair + 1) * LANES] = out.astype(BF16)


def _attention(q, kv, bias, sink_log2):
    bsz, s, _ = q.shape
    tq = min(ATT_TILE, s)
    nb = tq // ATT_BLOCK
    row = lambda b, i: (b, i, 0)
    return pl.pallas_call(
        _attention_kernel,
        grid=(bsz, s // tq),
        in_specs=[
            pl.BlockSpec((1, tq, ATT_Q), row),
            pl.BlockSpec((1, tq, 2 * ATT_KV), row),
            pl.BlockSpec((1, ATT_BLOCK, 2 * ATT_KV), lambda b, i: (b, jnp.maximum(i * nb - 1, 0), 0)),
            pl.BlockSpec((1, ATT_BLOCK, 2 * ATT_KV),
                         lambda b, i: (b, jnp.minimum((i + 1) * nb, s // ATT_BLOCK - 1), 0)),
            pl.BlockSpec(bias.shape, lambda b, i: (0, 0, 0, 0), pipeline_mode=pl.Buffered(1)),
            pl.BlockSpec(memory_space=pltpu.SMEM),
        ],
        out_specs=pl.BlockSpec((1, tq, ATT_Q), row),
        out_shape=jax.ShapeDtypeStruct((bsz, s, ATT_Q), BF16),
        scratch_shapes=[
            pltpu.VMEM((N_KV_HEADS, tq + 2 * ATT_BLOCK, LANES), BF16),
            pltpu.VMEM((N_KV_HEADS, tq + 2 * ATT_BLOCK, LANES), BF16),
            pltpu.VMEM((N_KV_HEADS, tq + 2 * ATT_BLOCK, LANES), BF16),
            pltpu.VMEM((N_KV_HEADS, tq + 2 * ATT_BLOCK, LANES), BF16),
        ],
        compiler_params=pltpu.CompilerParams(dimension_semantics=("parallel", "parallel"),
                                             vmem_limit_bytes=VMEM_LIMIT_BYTES),
        name="window_attention",
    )(q, kv, kv, kv, bias, sink_log2)


N_LEVELS = 6
CM_EYE, CM_NEG, CM_LEVEL0 = 0, 1, 2


def _dn_constants(reverse, ts):
    i = np.arange(CHUNK)[:, None]
    j = (np.arange(DN_W) % DN_DIM)[None, :]
    tri_incl = (i <= j) if reverse else (i >= j)
    rows = [(i == j).astype(np.float32), np.where(tri_incl, 0.0, -np.inf).astype(np.float32)]
    for lvl in range(1, N_LEVELS + 1):
        half = 1 << (lvl - 1)
        same_blk = (i >> lvl) == (j >> lvl)
        i_hi, j_hi = (i & half) != 0, (j & half) != 0
        rows.append((same_blk & ~i_hi & j_hi if reverse else same_blk & i_hi & ~j_hi).astype(np.float32))
    chunk_masks = np.stack(rows)
    r = np.arange(DN_W)
    mask_bd = (r[:, None] // DN_DIM) == (r[None, :] // DN_DIM)
    bd_f32 = np.stack([mask_bd.astype(np.float32), np.eye(DN_W, dtype=np.float32)])
    t = np.arange(ts)
    same_chunk = (t[:, None] // CHUNK) == (t[None, :] // CHUNK)
    cum = same_chunk & ((t[None, :] >= t[:, None]) if reverse else (t[None, :] <= t[:, None]))
    goff, boff = (DN_HEADS, 3 * DN_HEADS) if reverse else (0, 2 * DN_HEADS)
    er = np.arange(LANES)[:, None]
    ec = (np.arange(DN_W) // DN_DIM)[None, :]
    expand = np.stack([er == ec + goff, er == ec + boff])
    return (jnp.asarray(chunk_masks), jnp.asarray(bd_f32), jnp.asarray(mask_bd, BF16),
            jnp.asarray(cum, BF16), jnp.asarray(expand, BF16))


def _dn_tile(reverse, q, k, v, gcol, bcol, st, cm_ref, bdf_ref, bdb_ref):
    nchunk = q.shape[0] // CHUNK
    cs = range(nchunk)
    rows = [slice(c * CHUNK, (c + 1) * CHUNK) for c in cs]

    def bd(y):
        return jnp.concatenate([y] * DN_HEADS, axis=0) * bdb_ref[...]

    eye_p = cm_ref[CM_EYE]
    egc = jnp.exp(gcol)
    kbg = k * (bcol * egc)
    vb = v * bcol
    qe = q * egc
    gcs = [gcol[r] for r in rows]
    grow = [jnp.sum(eye_p * g, axis=0, keepdims=True) for g in gcs]
    gtot = [g[0:1, :] if reverse else g[CHUNK - 1:CHUNK, :] for g in gcs]
    decay = [jnp.exp(g - gr + cm_ref[CM_NEG]) for g, gr in zip(gcs, grow)]
    kb16 = [k[r].astype(BF16) for r in rows]
    gq = [_dot_nt(jnp.concatenate([kb, q[r].astype(BF16)], axis=0), bd(kb)) for kb, r in zip(kb16, rows)]
    lb = [g[:CHUNK] * bcol[r] * d for g, r, d in zip(gq, rows, decay)]
    ab = [(g[CHUNK:] * d).astype(BF16) for g, d in zip(gq, decay)]
    x = [eye_p - cm_ref[CM_LEVEL0] * l for l in lb]
    for lvl in range(1, N_LEVELS):
        xb = [xc.astype(BF16) for xc in x]
        z = [_dot((cm_ref[CM_LEVEL0 + lvl] * l).astype(BF16), bd(b)) for l, b in zip(lb, xb)]
        x = [xc - _dot(b, bd(zc.astype(BF16))) for xc, b, zc in zip(x, xb, z)]
    xb = [xc.astype(BF16) for xc in x]
    u = [_dot(b, bd(vb[r].astype(BF16))) for b, r in zip(xb, rows)]
    w = [_dot(b, bd(kbg[r].astype(BF16))) for b, r in zip(xb, rows)]
    p = [qe[r] - _dot(a, bd(wc.astype(BF16))) for r, a, wc in zip(rows, ab, w)]
    rr = [_dot(a, bd(uc.astype(BF16))) for a, uc in zip(ab, u)]
    kd = [(k[r] * jnp.exp(gt - g)).astype(BF16) for r, gt, g in zip(rows, gtot, gcs)]
    full = [_dot_tn(jnp.concatenate([wc, uc], axis=1).astype(BF16), kdc) for wc, uc, kdc in zip(w, u, kd)]
    bdm = [(bdf_ref[1] * jnp.exp(gt) - bdf_ref[0] * f[:DN_W]).astype(BF16) for gt, f in zip(gtot, full)]
    nfull = [bdf_ref[0] * f[DN_W:] for f in full]
    nt = [nf[0:64] + nf[64:128] + nf[128:192] + nf[192:256] for nf in nfull]
    outs = [None] * nchunk
    for c in (reversed(cs) if reverse else cs):
        sb = st.astype(BF16)
        outs[c] = _dot_nt(p[c].astype(BF16), bd(sb)) + rr[c]
        st = _dot(sb, bdm[c]) + nt[c]
    return outs, st


def _dn_kernel(reverse, x_ref, g_ref, gp_ref, cm_ref, bdf_ref, bdb_ref, cum_ref, exp_ref, *rest):
    if reverse:
        z_ref, of_ref, ng_ref, o_ref, st_ref = rest
    else:
        o_ref, st_ref = rest

    @pl.when(pl.program_id(1) == 0)
    def _():
        st_ref[...] = jnp.zeros(st_ref.shape, F32)

    qkv = x_ref[0].astype(F32)
    q = qkv[:, :DN_W]
    k = qkv[:, DN_W:2 * DN_W]
    v = qkv[:, 2 * DN_W:]

    gates = g_ref[0]
    gp = gp_ref[...]
    glog = -gp[0:1, :] * _softplus(gates + gp[1:2, :])
    beta = _sigmoid(gates)
    gcum = _dot_exact_rhs(cum_ref[...], glog)
    gcol = _dot_exact_lhs(gcum, exp_ref[0])
    bcol = _dot_exact_lhs(beta, exp_ref[1])

    outs, st = _dn_tile(reverse, q, k, v, gcol, bcol, st_ref[...], cm_ref, bdf_ref, bdb_ref)
    st_ref[...] = st
    o = jnp.concatenate(outs, axis=0)
    if reverse:
        o = o + of_ref[0]
        ms = _head_sumsq(o, bdb_ref[...]) * (1.0 / DN_DIM)
        y = o * lax.rsqrt(ms + RMS_EPS) * ng_ref[...] * _silu(z_ref[0].astype(F32))
        o_ref[0] = y.astype(BF16)
    else:
        o_ref[0] = o


def _delta_pass(reverse, dnqkv, gates, gate_params, z=None, o_fwd=None, norm_g=None):
    bsz, s, _ = dnqkv.shape
    ts = min(DN_TILE, s)
    n = s // ts
    pos = (lambda i: n - 1 - i) if reverse else (lambda i: i)
    row = lambda b, i: (b, pos(i), 0)
    full2 = lambda b, i: (0, 0)
    in_specs = [
        pl.BlockSpec((1, ts, 3 * DN_W), row),
        pl.BlockSpec((1, ts, LANES), row),
        pl.BlockSpec((F32_SUBLANES, LANES), full2),
    ]
    consts = _dn_constants(reverse, ts)
    in_specs += [pl.BlockSpec(c.shape, (lambda nd: lambda b, i: (0,) * nd)(c.ndim), pipeline_mode=pl.Buffered(1))
                 for c in consts]
    args = [dnqkv, gates, gate_params, *consts]
    if reverse:
        in_specs += [pl.BlockSpec((1, ts, DN_W), row), pl.BlockSpec((1, ts, DN_W), row),
                     pl.BlockSpec((1, DN_W), full2)]
        args += [z, o_fwd, norm_g]
        out_dtype = BF16
    else:
        out_dtype = F32
    return pl.pallas_call(
        functools.partial(_dn_kernel, reverse),
        grid=(bsz, n),
        in_specs=in_specs,
        out_specs=pl.BlockSpec((1, ts, DN_W), row),
        out_shape=jax.ShapeDtypeStruct((bsz, s, DN_W), out_dtype),
        scratch_shapes=[pltpu.VMEM((DN_DIM, DN_W), F32)],
        compiler_params=pltpu.CompilerParams(dimension_semantics=("parallel", "arbitrary"),
                                             vmem_limit_bytes=VMEM_LIMIT_BYTES),
        name="delta_bwd" if reverse else "delta_fwd",
    )(*args)


def _out_mlp_kernel(x_ref, ya_ref, yb_ref, yc_ref, g1_ref, sh2_ref, sc2_ref, g2_ref, wo_a_ref, wo_b_ref,
                    wo_c_ref, ln1g_ref, ln1b_ref, w1_ref, b1_ref, w2_ref, b2_ref, ln2g_ref, ln2b_ref, o_ref):
    x = x_ref[0]
    y = _dot(ya_ref[0], wo_a_ref[...]) + _dot(yb_ref[0], wo_b_ref[...]) + _dot(yc_ref[0], wo_c_ref[...])
    x1 = _layer_norm(ALPHA * x + (1.0 + g1_ref[0]) * y, ln1g_ref[...], ln1b_ref[...])
    h = (x1 * (1.0 + sc2_ref[0]) + sh2_ref[0]).astype(BF16)
    f = None
    for c in range(D_FF // FF_CHUNK):
        cols = slice(c * FF_CHUNK, (c + 1) * FF_CHUNK)
        a = jnp.maximum(_dot(h, w1_ref[:, cols]) + b1_ref[:, cols], 0.0)
        part = _dot((a * a).astype(BF16), w2_ref[cols, :])
        f = part if f is None else f + part
    f = f + b2_ref[...]
    o_ref[0] = _layer_norm(ALPHA * x1 + (1.0 + g2_ref[0]) * f, ln2g_ref[...], ln2b_ref[...])


def _out_mlp(x, ya, yb, yc, mod_l, b0, wo_a, wo_b, wo_c, ln1g, ln1b, w1, b1, w2, b2, ln2g, ln2b):
    bsz, s, _ = x.shape
    tm = min(ROW_TILE, s)
    row = lambda b, i: (b, i, 0)
    full2 = lambda b, i: (0, 0)
    modspec = lambda k: pl.BlockSpec((1, 1, D_MODEL), lambda b, i: (b0 + b, 0, k))
    const = lambda shape: pl.BlockSpec(shape, full2, pipeline_mode=pl.Buffered(1))
    return pl.pallas_call(
        _out_mlp_kernel,
        grid=(bsz, s // tm),
        in_specs=[
            pl.BlockSpec((1, tm, D_MODEL), row),
            pl.BlockSpec((1, tm, CONV_CH), row),
            pl.BlockSpec((1, tm, ATT_Q), row),
            pl.BlockSpec((1, tm, DN_W), row),
            modspec(2), modspec(3), modspec(4), modspec(5),
            const((CONV_CH, D_MODEL)), const((ATT_Q, D_MODEL)), const((DN_W, D_MODEL)),
            const((1, D_MODEL)), const((1, D_MODEL)),
            const((D_MODEL, D_FF)), const((1, D_FF)),
            const((D_FF, D_MODEL)), const((1, D_MODEL)),
            const((1, D_MODEL)), const((1, D_MODEL)),
        ],
        out_specs=pl.BlockSpec((1, tm, D_MODEL), row),
        out_shape=jax.ShapeDtypeStruct((bsz, s, D_MODEL), F32),
        compiler_params=pltpu.CompilerParams(dimension_semantics=("parallel", "parallel"),
                                             vmem_limit_bytes=VMEM_LIMIT_BYTES),
        name="out_mlp",
    )(x, ya, yb, yc, mod_l, mod_l, mod_l, mod_l, wo_a, wo_b, wo_c, ln1g, ln1b, w1, b1, w2, b2, ln2g, ln2b)


def _pad_rows(a, rows):
    return jnp.concatenate([a, jnp.zeros((rows - a.shape[0],) + a.shape[1:], a.dtype)], axis=0)


def _trunk(x, mod, b0, prm):
    for l in range(DEPTH):
        mod_l = mod[l]
        outs = _in_projection(x, mod_l, b0, prm["ln_in_g"], prm["ln_in_b"], prm["w_main"][l], prm["w_gate"][l],
                              prm["conv_a_w"][l], prm["dn_conv_w"][l], apply_ln=(l == 0))
        ya, q, kv, dnqkv, z, gates = outs[:6]
        if l == 0:
            x = outs[6]
        yb = _attention(q, kv, prm["attn_bias"], prm["sink_log2"][l])
        o_fwd = _delta_pass(False, dnqkv, gates, prm["gate_params"][l])
        yc = _delta_pass(True, dnqkv, gates, prm["gate_params"][l], z=z, o_fwd=o_fwd, norm_g=prm["dn_norm_g"][l])
        x = _out_mlp(x, ya, yb, yc, mod_l, b0, prm["wo_a"][l], prm["wo_b"][l], prm["wo_c"][l],
                     prm["ln1_g"][l], prm["ln1_b"][l], prm["w1"][l], prm["b1"][l], prm["w2"][l], prm["b2"][l],
                     prm["ln2_g"][l], prm["ln2_b"][l])
    return x


def kernel(x_prompt, x_sample, c_prompt, c_sample, ln_in_g, ln_in_b, w_mod, b_mod, w_in, conv_a_w, attn_sink,
           dn_conv_w, dn_a_log_f, dn_a_log_b, dn_dt_bias_f, dn_dt_bias_b, dn_norm_g, w_out, ln1_g, ln1_b,
           w1, b1, w2, b2, ln2_g, ln2_b):
    nb_p, nb_s = c_prompt.shape[0], c_sample.shape[0]
    bp = -(-(nb_p + nb_s) // BF16_SUBLANES) * BF16_SUBLANES
    c_all = _pad_rows(jnp.concatenate([c_prompt, c_sample], axis=0), bp)
    mod = _modulation(c_all, w_mod, b_mod)
    mod = mod.reshape(DEPTH, bp, 1, 6 * D_MODEL)

    row = lambda a: a.reshape(DEPTH, 1, -1)
    zeros4 = jnp.zeros((DEPTH, DN_HEADS), F32)
    gp0 = jnp.concatenate([jnp.exp(dn_a_log_f), jnp.exp(dn_a_log_b), zeros4, zeros4], axis=1)
    gp1 = jnp.concatenate([dn_dt_bias_f, dn_dt_bias_b, zeros4, zeros4], axis=1)
    gate_params = jnp.stack([gp0, gp1], axis=1)
    gate_params = jnp.pad(gate_params, ((0, 0), (0, F32_SUBLANES - 2), (0, LANES - 4 * DN_HEADS)))
    pad_taps = lambda w: jnp.pad(w, ((0, 0), (0, F32_SUBLANES - w.shape[1]), (0, 0)))
    prm = dict(
        ln_in_g=ln_in_g.reshape(1, -1), ln_in_b=ln_in_b.reshape(1, -1),
        w_main=w_in[:, :, :OFF_GATE].astype(BF16),
        w_gate=jnp.pad(w_in[:, :, OFF_GATE:], ((0, 0), (0, 0), (0, LANES - 4 * DN_HEADS))).astype(BF16),
        conv_a_w=pad_taps(conv_a_w), attn_bias=_attention_bias(), sink_log2=attn_sink * LOG2E,
        dn_conv_w=pad_taps(dn_conv_w), gate_params=gate_params,
        dn_norm_g=jnp.tile(dn_norm_g, (1, DN_HEADS)).reshape(DEPTH, 1, DN_W),
        wo_a=w_out[:, :CONV_CH, :].astype(BF16), wo_b=w_out[:, CONV_CH:CONV_CH + ATT_Q, :].astype(BF16),
        wo_c=w_out[:, CONV_CH + ATT_Q:, :].astype(BF16),
        ln1_g=row(ln1_g), ln1_b=row(ln1_b), w1=w1.astype(BF16), b1=row(b1), w2=w2.astype(BF16), b2=row(b2),
        ln2_g=row(ln2_g), ln2_b=row(ln2_b),
    )
    y_prompt = _trunk(x_prompt, mod, 0, prm)
    y_sample = _trunk(x_sample, mod, nb_p, prm)
    return (y_prompt, y_sample)
```

```python
import functools

import numpy as np
import jax
import jax.numpy as jnp
from jax import lax
from jax.experimental import pallas as pl
from jax.experimental.pallas import tpu as pltpu

F32 = jnp.float32
BF16 = jnp.bfloat16

D_MODEL = 1024
DEPTH = 4
CONV_CH = 256
N_HEADS = 8
N_KV_HEADS = 2
HEAD_DIM = 64
WINDOW = 128
DN_HEADS = 4
DN_DIM = 64
DN_W = DN_HEADS * DN_DIM
D_FF = 4 * D_MODEL
ATT_Q = N_HEADS * HEAD_DIM
ATT_KV = N_KV_HEADS * HEAD_DIM
OFF_ATT = 3 * CONV_CH
OFF_DN = OFF_ATT + ATT_Q + 2 * ATT_KV
OFF_GATE = OFF_DN + 4 * DN_W
D_IN = OFF_GATE + 4 * DN_HEADS
ALPHA = (2.0 * DEPTH) ** 0.25
LN_EPS = 1e-5
RMS_EPS = 1e-6
LOG2E = 1.4426950408889634

LANES = 128
BF16_SUBLANES = 16
F32_SUBLANES = 8
VMEM_LIMIT_BYTES = 56 * 1024 * 1024

ROW_TILE = 512
ATT_TILE = 512
ATT_BLOCK = 128
DN_TILE = 512
CHUNK = 64
CUM_ROWS = 256
FF_CHUNK = 1024
MLP_ROW_GROUPS = 2


def _dot(a, b):
    return jnp.dot(a, b, preferred_element_type=F32)


def _dot_nt(a, b):
    return lax.dot_general(a, b, (((1,), (1,)), ((), ())), preferred_element_type=F32)


def _dot_tn(a, b):
    return lax.dot_general(a, b, (((0,), (0,)), ((), ())), preferred_element_type=F32)


def _sigmoid(x):
    return 1.0 / (1.0 + jnp.exp(-x))


def _silu(x):
    return x * _sigmoid(x)


def _softplus(x):
    return jnp.maximum(x, 0.0) + jnp.log(1.0 + jnp.exp(-jnp.abs(x)))


def _layer_norm(x, g, b):
    mu = jnp.mean(x, axis=-1, keepdims=True)
    xc = x - mu
    var = jnp.mean(xc * xc, axis=-1, keepdims=True)
    return xc * lax.rsqrt(var + LN_EPS) * g + b


def _split3(x):
    hi = x.astype(BF16)
    r1 = x - hi.astype(F32)
    mid = r1.astype(BF16)
    lo = (r1 - mid.astype(F32)).astype(BF16)
    return hi, mid, lo


def _dot_exact_rhs(sel, x):
    hi, mid, lo = _split3(x)
    return _dot(sel, hi) + _dot(sel, mid) + _dot(sel, lo)


def _dot_exact_lhs(x, sel):
    hi, mid, lo = _split3(x)
    return _dot(hi, sel) + _dot(mid, sel) + _dot(lo, sel)


def _iota2(shape, axis):
    return lax.broadcasted_iota(jnp.int32, shape, axis)


def _mod_kernel(c_ref, w_ref, b_ref, o_ref):
    s = _silu(c_ref[...]).astype(BF16)
    o_ref[0] = _dot(s, w_ref[0].astype(BF16)) + b_ref[0]


def _modulation(c_all, w_mod, b_mod):
    bp = c_all.shape[0]
    nblk = w_mod.shape[2] // D_MODEL
    return pl.pallas_call(
        _mod_kernel,
        grid=(DEPTH, nblk),
        in_specs=[
            pl.BlockSpec((bp, D_MODEL), lambda l, k: (0, 0)),
            pl.BlockSpec((1, D_MODEL, D_MODEL), lambda l, k: (l, 0, k)),
            pl.BlockSpec((1, 1, D_MODEL), lambda l, k: (l, 0, k)),
        ],
        out_specs=pl.BlockSpec((1, bp, D_MODEL), lambda l, k: (l, 0, k)),
        out_shape=jax.ShapeDtypeStruct((DEPTH, bp, 6 * D_MODEL), F32),
        compiler_params=pltpu.CompilerParams(dimension_semantics=("arbitrary", "arbitrary"),
                                             vmem_limit_bytes=VMEM_LIMIT_BYTES),
        name="modulation",
    )(c_all, w_mod, b_mod.reshape(DEPTH, 1, 6 * D_MODEL))


def _conv3(ext, mid, taps, rows):
    return (taps[1:2, :] * mid + taps[0:1, :] * ext[F32_SUBLANES - 1:F32_SUBLANES - 1 + rows, :]
            + taps[2:3, :] * ext[F32_SUBLANES + 1:F32_SUBLANES + 1 + rows, :])


def _head_sumsq(y, ones_bd):
    y2 = y * y
    hi = y2.astype(BF16)
    lo = (y2 - hi.astype(F32)).astype(BF16)
    return _dot(hi, ones_bd) + _dot(lo, ones_bd)


def _inproj_kernel(apply_ln, x_ref, xp_ref, xn_ref, sh_ref, sc_ref, lng_ref, lnb_ref, w_ref, wg_ref,
                   cwa_ref, cwd_ref, bdb_ref, gp_ref, cum_ref, *rest):
    n_out = 7 if apply_ln else 6
    out_refs, (cu_ext, dn_ext) = rest[:n_out], rest[n_out:]
    o_ya, o_q, o_kv, o_dn, o_z, o_g = out_refs[:6]
    i = pl.program_id(1)
    n = pl.num_programs(1)
    tm = x_ref.shape[1]
    has_prev = (i > 0).astype(F32)
    has_next = (i < n - 1).astype(F32)
    x = x_ref[0]
    xh = jnp.concatenate([xp_ref[0], xn_ref[0]], axis=0)
    if apply_ln:
        x = _layer_norm(x, lng_ref[...], lnb_ref[...])
        xh = _layer_norm(xh, lng_ref[...], lnb_ref[...])
        out_refs[6][0] = x
    h = (x * (1.0 + sc_ref[0]) + sh_ref[0]).astype(BF16)
    hh = (xh * (1.0 + sc_ref[0]) + sh_ref[0]).astype(BF16)
    gates = _dot(h, wg_ref[...])
    gp = gp_ref[...]
    glog = -gp[0:1, :] * _softplus(gates + gp[1:2, :])
    beta = _sigmoid(gates)

    ph_d = _dot(hh, w_ref[:, OFF_DN:OFF_DN + 3 * DN_W])
    ph_a = _dot(hh, w_ref[:, CONV_CH:OFF_ATT])
    pd = _dot(h, w_ref[:, OFF_DN:OFF_DN + 3 * DN_W])

    lane = _iota2((CUM_ROWS, LANES), 1)
    for r0 in range(0, tm, CUM_ROWS):
        g = glog[r0:r0 + CUM_ROWS]
        pre = _dot_exact_rhs(cum_ref[...], g)
        tot = jnp.concatenate([jnp.broadcast_to(pre[c + CHUNK - 1:c + CHUNK], (CHUNK, LANES))
                               for c in range(0, CUM_ROWS, CHUNK)], axis=0)
        o_g[0, r0:r0 + CUM_ROWS, :] = jnp.where(
            lane < DN_HEADS, pre, jnp.where(lane < 2 * DN_HEADS, tot - pre + g, beta[r0:r0 + CUM_ROWS]))

    dn_ext[0:F32_SUBLANES, :] = ph_d[:F32_SUBLANES] * has_prev
    dn_ext[F32_SUBLANES:F32_SUBLANES + tm, :] = pd
    dn_ext[F32_SUBLANES + tm:, :] = ph_d[F32_SUBLANES:] * has_next
    ones_bd = bdb_ref[...]
    half = tm // 2

    def dn_post(r0):
        mid = dn_ext[F32_SUBLANES + r0:F32_SUBLANES + r0 + half, :]
        qkv = _silu(_conv3(dn_ext.at[pl.ds(r0, half + 2 * F32_SUBLANES)], mid, cwd_ref[...], half))
        q = qkv[:, :DN_W]
        k = qkv[:, DN_W:2 * DN_W]
        rows = pl.ds(r0, half)
        o_dn[0, rows, :DN_W] = (q * (lax.rsqrt(_head_sumsq(q, ones_bd) + RMS_EPS) * (DN_DIM ** -0.5))).astype(BF16)
        o_dn[0, rows, DN_W:2 * DN_W] = (k * lax.rsqrt(_head_sumsq(k, ones_bd) + RMS_EPS)).astype(BF16)
        o_dn[0, rows, 2 * DN_W:] = qkv[:, 2 * DN_W:].astype(BF16)

    dn_post(0)
    pa = _dot(h, w_ref[:, :OFF_ATT])
    dn_post(half)
    o_q[0] = (_dot(h, w_ref[:, OFF_ATT:OFF_ATT + ATT_Q]) * (HEAD_DIM ** -0.5 * LOG2E)).astype(BF16)

    cu = pa[:, CONV_CH:2 * CONV_CH] * pa[:, 2 * CONV_CH:]
    cuh = ph_a[:, :CONV_CH] * ph_a[:, CONV_CH:]
    cu_ext[0:F32_SUBLANES, :] = cuh[:F32_SUBLANES] * has_prev
    cu_ext[F32_SUBLANES:F32_SUBLANES + tm, :] = cu
    cu_ext[F32_SUBLANES + tm:, :] = cuh[F32_SUBLANES:] * has_next
    o_ya[0] = (pa[:, :CONV_CH] * _conv3(cu_ext, cu, cwa_ref[...], tm)).astype(BF16)

    o_kv[0] = _dot(h, w_ref[:, OFF_ATT + ATT_Q:OFF_DN]).astype(BF16)
    o_z[0] = _dot(h, w_ref[:, OFF_DN + 3 * DN_W:OFF_GATE]).astype(BF16)


def _block_diag_ones():
    r = np.arange(DN_W)
    return jnp.asarray((r[:, None] // DN_DIM) == (r[None, :] // DN_DIM), BF16)


def _chunk_cumsum_selector():
    t = np.arange(CUM_ROWS)
    same_chunk = (t[:, None] // CHUNK) == (t[None, :] // CHUNK)
    return jnp.asarray(same_chunk & (t[None, :] <= t[:, None]), BF16)


def _in_projection(x, mod_l, b0, ln_g, ln_b, w_main, w_gate, conv_a_w, dn_conv_w, gate_params, apply_ln):
    bsz, s, _ = x.shape
    tm = min(ROW_TILE, s)
    assert s % tm == 0 and tm % CUM_ROWS == 0, (s, tm)
    nh = tm // F32_SUBLANES
    row = lambda b, i: (b, i, 0)
    full2 = lambda b, i: (0, 0)
    const = lambda shape: pl.BlockSpec(shape, full2, pipeline_mode=pl.Buffered(1))
    widths = (CONV_CH, ATT_Q, 2 * ATT_KV, 3 * DN_W, DN_W, LANES)
    dtypes = (BF16, BF16, BF16, BF16, BF16, F32)
    out_shape = [jax.ShapeDtypeStruct((bsz, s, w), dt) for w, dt in zip(widths, dtypes)]
    out_specs = [pl.BlockSpec((1, tm, w), row) for w in widths]
    if apply_ln:
        out_shape.append(jax.ShapeDtypeStruct((bsz, s, D_MODEL), F32))
        out_specs.append(pl.BlockSpec((1, tm, D_MODEL), row))
    return pl.pallas_call(
        functools.partial(_inproj_kernel, apply_ln),
        grid=(bsz, s // tm),
        in_specs=[
            pl.BlockSpec((1, tm, D_MODEL), row),
            pl.BlockSpec((1, F32_SUBLANES, D_MODEL), lambda b, i: (b, jnp.maximum(i * nh - 1, 0), 0)),
            pl.BlockSpec((1, F32_SUBLANES, D_MODEL),
                         lambda b, i: (b, jnp.minimum((i + 1) * nh, s // F32_SUBLANES - 1), 0)),
            pl.BlockSpec((1, 1, D_MODEL), lambda b, i: (b0 + b, 0, 0)),
            pl.BlockSpec((1, 1, D_MODEL), lambda b, i: (b0 + b, 0, 1)),
            pl.BlockSpec((1, D_MODEL), full2),
            pl.BlockSpec((1, D_MODEL), full2),
            const((D_MODEL, OFF_GATE)),
            const((D_MODEL, LANES)),
            const((F32_SUBLANES, CONV_CH)),
            const((F32_SUBLANES, 3 * DN_W)),
            const((DN_W, DN_W)),
            const((F32_SUBLANES, LANES)),
            const((CUM_ROWS, CUM_ROWS)),
        ],
        out_specs=out_specs,
        out_shape=out_shape,
        scratch_shapes=[pltpu.VMEM((tm + 2 * F32_SUBLANES, CONV_CH), F32),
                        pltpu.VMEM((tm + 2 * F32_SUBLANES, 3 * DN_W), F32)],
        compiler_params=pltpu.CompilerParams(dimension_semantics=("parallel", "parallel"),
                                             vmem_limit_bytes=VMEM_LIMIT_BYTES),
        name="in_projection",
    )(x, x, x, mod_l, mod_l, ln_g, ln_b, w_main, w_gate, conv_a_w, dn_conv_w, _block_diag_ones(), gate_params,
      _chunk_cumsum_selector())


def _attention_bias():
    qi = jnp.arange(ATT_BLOCK)[:, None]
    ki = jnp.arange(3 * ATT_BLOCK)[None, :]
    dist = jnp.abs(ki - ATT_BLOCK - qi)
    slopes = jnp.exp2(-8.0 * jnp.arange(1, N_HEADS + 1, dtype=F32) / N_HEADS)
    base = -(slopes * LOG2E)[:, None, None] * dist.astype(F32)[None]
    variants = []
    for var in range(4):
        valid = dist <= WINDOW
        if var & 1:
            valid = valid & (ki >= ATT_BLOCK)
        if var & 2:
            valid = valid & (ki < 2 * ATT_BLOCK)
        variants.append(jnp.where(valid[None], base, -jnp.inf))
    return jnp.stack(variants)


def _attention_kernel(q_ref, kv_ref, kv_prev_ref, kv_next_ref, bias_ref, sink_ref, o_ref,
                      ka_ref, kb_ref, va_ref, vb_ref):
    i = pl.program_id(1)
    n = pl.num_programs(1)
    tq = q_ref.shape[1]

    def fill(r0, blk):
        kk = blk[:, :LANES].astype(F32)
        vv = blk[:, LANES:].astype(F32)
        lo = _iota2(kk.shape, 1) < HEAD_DIM
        kks = pltpu.roll(kk, HEAD_DIM, 1)
        vvs = pltpu.roll(vv, HEAD_DIM, 1)
        rows = pl.ds(r0, blk.shape[0])
        ka_ref[0, rows, :] = jnp.where(lo, kk, 0.0).astype(BF16)
        kb_ref[0, rows, :] = jnp.where(lo, 0.0, kks).astype(BF16)
        ka_ref[1, rows, :] = jnp.where(lo, kks, 0.0).astype(BF16)
        kb_ref[1, rows, :] = jnp.where(lo, 0.0, kk).astype(BF16)
        va_ref[0, rows, :] = jnp.where(lo, vv, 0.0).astype(BF16)
        vb_ref[0, rows, :] = jnp.where(lo, 0.0, vvs).astype(BF16)
        va_ref[1, rows, :] = jnp.where(lo, vvs, 0.0).astype(BF16)
        vb_ref[1, rows, :] = jnp.where(lo, 0.0, vv).astype(BF16)

    fill(0, kv_prev_ref[0])
    fill(ATT_BLOCK, kv_ref[0])
    fill(ATT_BLOCK + tq, kv_next_ref[0])

    nwin = 3 * ATT_BLOCK
    nsub = tq // ATT_BLOCK
    lane_lo = _iota2((ATT_BLOCK, LANES), 1) < HEAD_DIM
    grp = N_HEADS // N_KV_HEADS

    heads = range(N_HEADS)
    for j in range(nsub):
        blk = i * nsub + j
        var = jnp.where(blk == 0, 1, 0) + jnp.where(blk == n * nsub - 1, 2, 0)
        win = pl.ds(j * ATT_BLOCK, nwin)
        qrows = pl.ds(j * ATT_BLOCK, ATT_BLOCK)
        kz = [(kb_ref if h % 2 else ka_ref)[h // grp, win, :] for h in heads]
        vz = [(vb_ref if h % 2 else va_ref)[h // grp, win, :] for h in heads]
        qp = [q_ref[0, qrows, (h // 2) * LANES:(h // 2 + 1) * LANES] for h in heads]
        sc = [_dot_nt(qp[h], kz[h]) + bias_ref[var, h] for h in heads]
        m = [jnp.maximum(jnp.max(sc[h], axis=-1, keepdims=True), sink_ref[h]) for h in heads]
        e = [jnp.exp2(sc[h] - m[h]) for h in heads]
        rcp = [1.0 / (jnp.sum(e[h], axis=-1, keepdims=True) + jnp.exp2(sink_ref[h] - m[h])) for h in heads]
        pv = [_dot(e[h].astype(BF16), vz[h]) for h in heads]
        for pair in range(N_HEADS // 2):
            out = (pv[2 * pair] + pv[2 * pair + 1]) * jnp.where(lane_lo, rcp[2 * pair], rcp[2 * pair + 1])
            o_ref[0, qrows, pair * LANES:(pair + 1) * LANES] = out.astype(BF16)


def _attention(q, kv, bias, sink_log2):
    bsz, s, _ = q.shape
    tq = min(ATT_TILE, s)
    nb = tq // ATT_BLOCK
    row = lambda b, i: (b, i, 0)
    return pl.pallas_call(
        _attention_kernel,
        grid=(bsz, s // tq),
        in_specs=[
            pl.BlockSpec((1, tq, ATT_Q), row),
            pl.BlockSpec((1, tq, 2 * ATT_KV), row),
            pl.BlockSpec((1, ATT_BLOCK, 2 * ATT_KV), lambda b, i: (b, jnp.maximum(i * nb - 1, 0), 0)),
            pl.BlockSpec((1, ATT_BLOCK, 2 * ATT_KV),
                         lambda b, i: (b, jnp.minimum((i + 1) * nb, s // ATT_BLOCK - 1), 0)),
            pl.BlockSpec(bias.shape, lambda b, i: (0, 0, 0, 0), pipeline_mode=pl.Buffered(1)),
            pl.BlockSpec(memory_space=pltpu.SMEM),
        ],
        out_specs=pl.BlockSpec((1, tq, ATT_Q), row),
        out_shape=jax.ShapeDtypeStruct((bsz, s, ATT_Q), BF16),
        scratch_shapes=[
            pltpu.VMEM((N_KV_HEADS, tq + 2 * ATT_BLOCK, LANES), BF16),
            pltpu.VMEM((N_KV_HEADS, tq + 2 * ATT_BLOCK, LANES), BF16),
            pltpu.VMEM((N_KV_HEADS, tq + 2 * ATT_BLOCK, LANES), BF16),
            pltpu.VMEM((N_KV_HEADS, tq + 2 * ATT_BLOCK, LANES), BF16),
        ],
        compiler_params=pltpu.CompilerParams(dimension_semantics=("parallel", "parallel"),
                                             vmem_limit_bytes=VMEM_LIMIT_BYTES),
        name="window_attention",
    )(q, kv, kv, kv, bias, sink_log2)


N_LEVELS = 6
CM_EYE, CM_NEG, CM_LEVEL0 = 0, 1, 2


def _dn_constants(reverse, ts):
    i = np.arange(CHUNK)[:, None]
    j = (np.arange(DN_W) % DN_DIM)[None, :]
    tri_incl = (i <= j) if reverse else (i >= j)
    rows = [(i == j).astype(np.float32), np.where(tri_incl, 0.0, -np.inf).astype(np.float32)]
    for lvl in range(1, N_LEVELS + 1):
        half = 1 << (lvl - 1)
        same_blk = (i >> lvl) == (j >> lvl)
        i_hi, j_hi = (i & half) != 0, (j & half) != 0
        rows.append((same_blk & ~i_hi & j_hi if reverse else same_blk & i_hi & ~j_hi).astype(np.float32))
    chunk_masks = np.stack(rows)
    r = np.arange(DN_W)
    mask_bd = (r[:, None] // DN_DIM) == (r[None, :] // DN_DIM)
    bd_f32 = np.stack([mask_bd.astype(np.float32), np.eye(DN_W, dtype=np.float32)])
    goff, boff = (DN_HEADS, 3 * DN_HEADS) if reverse else (0, 2 * DN_HEADS)
    er = np.arange(LANES)[:, None]
    ec = (np.arange(DN_W) // DN_DIM)[None, :]
    expand = np.stack([er == ec + goff, er == ec + boff])
    return (jnp.asarray(chunk_masks), jnp.asarray(bd_f32), jnp.asarray(mask_bd, BF16),
            jnp.asarray(expand, BF16))


def _dn_tile(reverse, q, k, v, gfeat, st, cm_ref, bdf_ref, bdb_ref, exp_ref):
    nchunk = q.shape[0] // CHUNK
    cs = range(nchunk)
    rows = [slice(c * CHUNK, (c + 1) * CHUNK) for c in cs]

    def bd(y):
        return jnp.concatenate([y] * DN_HEADS, axis=0) * bdb_ref[...]

    kb16 = [k[r].astype(BF16) for r in rows]
    gq = [_dot_nt(jnp.concatenate([kb, q[r].astype(BF16)], axis=0), bd(kb)) for kb, r in zip(kb16, rows)]
    gcol = _dot_exact_lhs(gfeat, exp_ref[0])
    bcol = _dot_exact_lhs(gfeat, exp_ref[1])
    eye_p = cm_ref[CM_EYE]
    egc = jnp.exp(gcol)
    kbg = k * (bcol * egc)
    vb = v * bcol
    qe = q * egc
    gcs = [gcol[r] for r in rows]
    grow = [jnp.sum(eye_p * g, axis=0, keepdims=True) for g in gcs]
    gtot = [g[0:1, :] if reverse else g[CHUNK - 1:CHUNK, :] for g in gcs]
    decay = [jnp.exp(g - gr + cm_ref[CM_NEG]) for g, gr in zip(gcs, grow)]
    lb = [g[:CHUNK] * bcol[r] * d for g, r, d in zip(gq, rows, decay)]
    ab = [(g[CHUNK:] * d).astype(BF16) for g, d in zip(gq, decay)]
    x = [eye_p - cm_ref[CM_LEVEL0] * l for l in lb]
    for lvl in range(1, N_LEVELS):
        xb = [xc.astype(BF16) for xc in x]
        z = [_dot((cm_ref[CM_LEVEL0 + lvl] * l).astype(BF16), bd(b)) for l, b in zip(lb, xb)]
        x = [xc - _dot(b, bd(zc.astype(BF16))) for xc, b, zc in zip(x, xb, z)]
    xb = [xc.astype(BF16) for xc in x]
    u = [_dot(b, bd(vb[r].astype(BF16))) for b, r in zip(xb, rows)]
    w = [_dot(b, bd(kbg[r].astype(BF16))) for b, r in zip(xb, rows)]
    p = [qe[r] - _dot(a, bd(wc.astype(BF16))) for r, a, wc in zip(rows, ab, w)]
    rr = [_dot(a, bd(uc.astype(BF16))) for a, uc in zip(ab, u)]
    kd = [(k[r] * jnp.exp(gt - g)).astype(BF16) for r, gt, g in zip(rows, gtot, gcs)]
    full = [_dot_tn(jnp.concatenate([wc, uc], axis=1).astype(BF16), kdc) for wc, uc, kdc in zip(w, u, kd)]
    bdm = [(bdf_ref[1] * jnp.exp(gt) - bdf_ref[0] * f[:DN_W]).astype(BF16) for gt, f in zip(gtot, full)]
    nfull = [bdf_ref[0] * f[DN_W:] for f in full]
    nt = [nf[0:64] + nf[64:128] + nf[128:192] + nf[192:256] for nf in nfull]
    outs = [None] * nchunk
    for c in (reversed(cs) if reverse else cs):
        sb = st.astype(BF16)
        outs[c] = _dot_nt(p[c].astype(BF16), bd(sb)) + rr[c]
        st = _dot(sb, bdm[c]) + nt[c]
    return outs, st


def _dn_kernel(reverse, x_ref, g_ref, cm_ref, bdf_ref, bdb_ref, exp_ref, *rest):
    if reverse:
        z_ref, of_ref, ng_ref, o_ref, st_ref = rest
    else:
        o_ref, st_ref = rest

    @pl.when(pl.program_id(1) == 0)
    def _():
        st_ref[...] = jnp.zeros(st_ref.shape, F32)

    qkv = x_ref[0].astype(F32)
    q = qkv[:, :DN_W]
    k = qkv[:, DN_W:2 * DN_W]
    v = qkv[:, 2 * DN_W:]

    outs, st = _dn_tile(reverse, q, k, v, g_ref[0], st_ref[...], cm_ref, bdf_ref, bdb_ref, exp_ref)
    st_ref[...] = st
    o = jnp.concatenate(outs, axis=0)
    if reverse:
        o = o + of_ref[0]
        ms = _head_sumsq(o, bdb_ref[...]) * (1.0 / DN_DIM)
        y = o * lax.rsqrt(ms + RMS_EPS) * ng_ref[...] * _silu(z_ref[0].astype(F32))
        o_ref[0] = y.astype(BF16)
    else:
        o_ref[0] = o


def _delta_pass(reverse, dnqkv, gfeat, z=None, o_fwd=None, norm_g=None):
    bsz, s, _ = dnqkv.shape
    ts = min(DN_TILE, s)
    n = s // ts
    pos = (lambda i: n - 1 - i) if reverse else (lambda i: i)
    row = lambda b, i: (b, pos(i), 0)
    full2 = lambda b, i: (0, 0)
    in_specs = [
        pl.BlockSpec((1, ts, 3 * DN_W), row),
        pl.BlockSpec((1, ts, LANES), row),
    ]
    consts = _dn_constants(reverse, ts)
    in_specs += [pl.BlockSpec(c.shape, (lambda nd: lambda b, i: (0,) * nd)(c.ndim), pipeline_mode=pl.Buffered(1))
                 for c in consts]
    args = [dnqkv, gfeat, *consts]
    if reverse:
        in_specs += [pl.BlockSpec((1, ts, DN_W), row), pl.BlockSpec((1, ts, DN_W), row),
                     pl.BlockSpec((1, DN_W), full2)]
        args += [z, o_fwd, norm_g]
        out_dtype = BF16
    else:
        out_dtype = F32
    return pl.pallas_call(
        functools.partial(_dn_kernel, reverse),
        grid=(bsz, n),
        in_specs=in_specs,
        out_specs=pl.BlockSpec((1, ts, DN_W), row),
        out_shape=jax.ShapeDtypeStruct((bsz, s, DN_W), out_dtype),
        scratch_shapes=[pltpu.VMEM((DN_DIM, DN_W), F32)],
        compiler_params=pltpu.CompilerParams(dimension_semantics=("parallel", "arbitrary"),
                                             vmem_limit_bytes=VMEM_LIMIT_BYTES),
        name="delta_bwd" if reverse else "delta_fwd",
    )(*args)


def _out_mlp_kernel(x_ref, ya_ref, yb_ref, yc_ref, g1_ref, sh2_ref, sc2_ref, g2_ref, wo_a_ref, wo_b_ref,
                    wo_c_ref, ln1g_ref, ln1b_ref, w1_ref, b1_ref, w2_ref, b2_ref, ln2g_ref, ln2b_ref, o_ref):
    tm = x_ref.shape[1]
    groups = [pl.ds(g * (tm // MLP_ROW_GROUPS), tm // MLP_ROW_GROUPS) for g in range(MLP_ROW_GROUPS)]
    y = [_dot(ya_ref[0, r, :], wo_a_ref[...]) + _dot(yb_ref[0, r, :], wo_b_ref[...])
         + _dot(yc_ref[0, r, :], wo_c_ref[...]) for r in groups]
    x1 = [_layer_norm(ALPHA * x_ref[0, r, :] + (1.0 + g1_ref[0]) * yg, ln1g_ref[...], ln1b_ref[...])
          for r, yg in zip(groups, y)]
    h = [(xg * (1.0 + sc2_ref[0]) + sh2_ref[0]).astype(BF16) for xg in x1]
    f = [None] * MLP_ROW_GROUPS
    for c in range(D_FF // FF_CHUNK):
        cols = slice(c * FF_CHUNK, (c + 1) * FF_CHUNK)
        a = [jnp.maximum(_dot(hg, w1_ref[:, cols]) + b1_ref[:, cols], 0.0) for hg in h]
        part = [_dot((ag * ag).astype(BF16), w2_ref[cols, :]) for ag in a]
        f = [pg if fg is None else fg + pg for fg, pg in zip(f, part)]
    for r, xg, fg in zip(groups, x1, f):
        o_ref[0, r, :] = _layer_norm(ALPHA * xg + (1.0 + g2_ref[0]) * (fg + b2_ref[...]),
                                     ln2g_ref[...], ln2b_ref[...])


def _out_mlp(x, ya, yb, yc, mod_l, b0, wo_a, wo_b, wo_c, ln1g, ln1b, w1, b1, w2, b2, ln2g, ln2b):
    bsz, s, _ = x.shape
    tm = min(ROW_TILE, s)
    row = lambda b, i: (b, i, 0)
    full2 = lambda b, i: (0, 0)
    modspec = lambda k: pl.BlockSpec((1, 1, D_MODEL), lambda b, i: (b0 + b, 0, k))
    const = lambda shape: pl.BlockSpec(shape, full2, pipeline_mode=pl.Buffered(1))
    return pl.pallas_call(
        _out_mlp_kernel,
        grid=(bsz, s // tm),
        in_specs=[
            pl.BlockSpec((1, tm, D_MODEL), row),
            pl.BlockSpec((1, tm, CONV_CH), row),
            pl.BlockSpec((1, tm, ATT_Q), row),
            pl.BlockSpec((1, tm, DN_W), row),
            modspec(2), modspec(3), modspec(4), modspec(5),
            const((CONV_CH, D_MODEL)), const((ATT_Q, D_MODEL)), const((DN_W, D_MODEL)),
            const((1, D_MODEL)), const((1, D_MODEL)),
            const((D_MODEL, D_FF)), const((1, D_FF)),
            const((D_FF, D_MODEL)), const((1, D_MODEL)),
            const((1, D_MODEL)), const((1, D_MODEL)),
        ],
        out_specs=pl.BlockSpec((1, tm, D_MODEL), row),
        out_shape=jax.ShapeDtypeStruct((bsz, s, D_MODEL), F32),
        compiler_params=pltpu.CompilerParams(dimension_semantics=("parallel", "parallel"),
                                             vmem_limit_bytes=VMEM_LIMIT_BYTES),
        name="out_mlp",
    )(x, ya, yb, yc, mod_l, mod_l, mod_l, mod_l, wo_a, wo_b, wo_c, ln1g, ln1b, w1, b1, w2, b2, ln2g, ln2b)


def _pad_rows(a, rows):
    return jnp.concatenate([a, jnp.zeros((rows - a.shape[0],) + a.shape[1:], a.dtype)], axis=0)


def _trunk(x, mod, b0, prm):
    for l in range(DEPTH):
        mod_l = mod[l]
        outs = _in_projection(x, mod_l, b0, prm["ln_in_g"], prm["ln_in_b"], prm["w_main"][l], prm["w_gate"][l],
                              prm["conv_a_w"][l], prm["dn_conv_w"][l], prm["gate_params"][l], apply_ln=(l == 0))
        ya, q, kv, dnqkv, z, gfeat = outs[:6]
        if l == 0:
            x = outs[6]
        yb = _attention(q, kv, prm["attn_bias"], prm["sink_log2"][l])
        o_fwd = _delta_pass(False, dnqkv, gfeat)
        yc = _delta_pass(True, dnqkv, gfeat, z=z, o_fwd=o_fwd, norm_g=prm["dn_norm_g"][l])
        x = _out_mlp(x, ya, yb, yc, mod_l, b0, prm["wo_a"][l], prm["wo_b"][l], prm["wo_c"][l],
                     prm["ln1_g"][l], prm["ln1_b"][l], prm["w1"][l], prm["b1"][l], prm["w2"][l], prm["b2"][l],
                     prm["ln2_g"][l], prm["ln2_b"][l])
    return x


def kernel(x_prompt, x_sample, c_prompt, c_sample, ln_in_g, ln_in_b, w_mod, b_mod, w_in, conv_a_w, attn_sink,
           dn_conv_w, dn_a_log_f, dn_a_log_b, dn_dt_bias_f, dn_dt_bias_b, dn_norm_g, w_out, ln1_g, ln1_b,
           w1, b1, w2, b2, ln2_g, ln2_b):
    nb_p, nb_s = c_prompt.shape[0], c_sample.shape[0]
    bp = -(-(nb_p + nb_s) // BF16_SUBLANES) * BF16_SUBLANES
    c_all = _pad_rows(jnp.concatenate([c_prompt, c_sample], axis=0), bp)
    mod = _modulation(c_all, w_mod, b_mod)
    mod = mod.reshape(DEPTH, bp, 1, 6 * D_MODEL)

    row = lambda a: a.reshape(DEPTH, 1, -1)
    zeros4 = jnp.zeros((DEPTH, DN_HEADS), F32)
    gp0 = jnp.concatenate([jnp.exp(dn_a_log_f), jnp.exp(dn_a_log_b), zeros4, zeros4], axis=1)
    gp1 = jnp.concatenate([dn_dt_bias_f, dn_dt_bias_b, zeros4, zeros4], axis=1)
    gate_params = jnp.stack([gp0, gp1], axis=1)
    gate_params = jnp.pad(gate_params, ((0, 0), (0, F32_SUBLANES - 2), (0, LANES - 4 * DN_HEADS)))
    pad_taps = lambda w: jnp.pad(w, ((0, 0), (0, F32_SUBLANES - w.shape[1]), (0, 0)))
    prm = dict(
        ln_in_g=ln_in_g.reshape(1, -1), ln_in_b=ln_in_b.reshape(1, -1),
        w_main=w_in[:, :, :OFF_GATE].astype(BF16),
        w_gate=jnp.pad(w_in[:, :, OFF_GATE:], ((0, 0), (0, 0), (0, LANES - 4 * DN_HEADS))).astype(BF16),
        conv_a_w=pad_taps(conv_a_w), attn_bias=_attention_bias(), sink_log2=attn_sink * LOG2E,
        dn_conv_w=pad_taps(dn_conv_w), gate_params=gate_params,
        dn_norm_g=jnp.tile(dn_norm_g, (1, DN_HEADS)).reshape(DEPTH, 1, DN_W),
        wo_a=w_out[:, :CONV_CH, :].astype(BF16), wo_b=w_out[:, CONV_CH:CONV_CH + ATT_Q, :].astype(BF16),
        wo_c=w_out[:, CONV_CH + ATT_Q:, :].astype(BF16),
        ln1_g=row(ln1_g), ln1_b=row(ln1_b), w1=w1.astype(BF16), b1=row(b1), w2=w2.astype(BF16), b2=row(b2),
        ln2_g=row(ln2_g), ln2_b=row(ln2_b),
    )
    y_prompt = _trunk(x_prompt, mod, 0, prm)
    y_sample = _trunk(x_sample, mod, nb_p, prm)
    return (y_prompt, y_sample)
```

```python
import functools

import numpy as np
import jax
import jax.numpy as jnp
from jax import lax
from jax.experimental import pallas as pl
from jax.experimental.pallas import tpu as pltpu

F32 = jnp.float32
BF16 = jnp.bfloat16

D_MODEL = 1024
DEPTH = 4
CONV_CH = 256
N_HEADS = 8
N_KV_HEADS = 2
HEAD_DIM = 64
WINDOW = 128
DN_HEADS = 4
DN_DIM = 64
DN_W = DN_HEADS * DN_DIM
D_FF = 4 * D_MODEL
ATT_Q = N_HEADS * HEAD_DIM
ATT_KV = N_KV_HEADS * HEAD_DIM
OFF_ATT = 3 * CONV_CH
OFF_DN = OFF_ATT + ATT_Q + 2 * ATT_KV
OFF_GATE = OFF_DN + 4 * DN_W
D_IN = OFF_GATE + 4 * DN_HEADS
ALPHA = (2.0 * DEPTH) ** 0.25
LN_EPS = 1e-5
RMS_EPS = 1e-6
LOG2E = 1.4426950408889634

LANES = 128
BF16_SUBLANES = 16
F32_SUBLANES = 8
VMEM_LIMIT_BYTES = 56 * 1024 * 1024

ROW_TILE = 512
ATT_TILE = 512
ATT_BLOCK = 128
DN_TILE = 512
CHUNK = 64
CUM_ROWS = 256
FF_CHUNK = 1024
MLP_ROW_GROUPS = 2


def _dot(a, b):
    return jnp.dot(a, b, preferred_element_type=F32)


def _dot_nt(a, b):
    return lax.dot_general(a, b, (((1,), (1,)), ((), ())), preferred_element_type=F32)


def _dot_tn(a, b):
    return lax.dot_general(a, b, (((0,), (0,)), ((), ())), preferred_element_type=F32)


def _sigmoid(x):
    return 1.0 / (1.0 + jnp.exp(-x))


def _silu(x):
    return x * _sigmoid(x)


def _softplus(x):
    return jnp.maximum(x, 0.0) + jnp.log(1.0 + jnp.exp(-jnp.abs(x)))


def _layer_norm(x, g, b):
    mu = jnp.mean(x, axis=-1, keepdims=True)
    xc = x - mu
    var = jnp.mean(xc * xc, axis=-1, keepdims=True)
    return xc * lax.rsqrt(var + LN_EPS) * g + b


def _split3(x):
    hi = x.astype(BF16)
    r1 = x - hi.astype(F32)
    mid = r1.astype(BF16)
    lo = (r1 - mid.astype(F32)).astype(BF16)
    return hi, mid, lo


def _dot_exact_rhs(sel, x):
    hi, mid, lo = _split3(x)
    return _dot(sel, hi) + _dot(sel, mid) + _dot(sel, lo)


def _dot_exact_lhs(x, sel):
    hi, mid, lo = _split3(x)
    return _dot(hi, sel) + _dot(mid, sel) + _dot(lo, sel)


def _iota2(shape, axis):
    return lax.broadcasted_iota(jnp.int32, shape, axis)


def _mod_kernel(c_ref, w_ref, b_ref, o_ref):
    s = _silu(c_ref[...]).astype(BF16)
    o_ref[0] = _dot(s, w_ref[0].astype(BF16)) + b_ref[0]


def _modulation(c_all, w_mod, b_mod):
    bp = c_all.shape[0]
    nblk = w_mod.shape[2] // D_MODEL
    return pl.pallas_call(
        _mod_kernel,
        grid=(DEPTH, nblk),
        in_specs=[
            pl.BlockSpec((bp, D_MODEL), lambda l, k: (0, 0)),
            pl.BlockSpec((1, D_MODEL, D_MODEL), lambda l, k: (l, 0, k)),
            pl.BlockSpec((1, 1, D_MODEL), lambda l, k: (l, 0, k)),
        ],
        out_specs=pl.BlockSpec((1, bp, D_MODEL), lambda l, k: (l, 0, k)),
        out_shape=jax.ShapeDtypeStruct((DEPTH, bp, 6 * D_MODEL), F32),
        compiler_params=pltpu.CompilerParams(dimension_semantics=("arbitrary", "arbitrary"),
                                             vmem_limit_bytes=VMEM_LIMIT_BYTES),
        name="modulation",
    )(c_all, w_mod, b_mod.reshape(DEPTH, 1, 6 * D_MODEL))


def _conv3(ext, mid, taps, rows):
    return (taps[1:2, :] * mid + taps[0:1, :] * ext[F32_SUBLANES - 1:F32_SUBLANES - 1 + rows, :]
            + taps[2:3, :] * ext[F32_SUBLANES + 1:F32_SUBLANES + 1 + rows, :])


def _head_sumsq(y, ones_bd):
    y2 = y * y
    hi = y2.astype(BF16)
    lo = (y2 - hi.astype(F32)).astype(BF16)
    return _dot(hi, ones_bd) + _dot(lo, ones_bd)


def _inproj_kernel(apply_ln, x_ref, xp_ref, xn_ref, sh_ref, sc_ref, lng_ref, lnb_ref, w_ref, wg_ref,
                   cwa_ref, cwd_ref, bdb_ref, gp_ref, cum_ref, *rest):
    n_out = 7 if apply_ln else 6
    out_refs, (cu_ext, dn_ext) = rest[:n_out], rest[n_out:]
    o_ya, o_q, o_kv, o_dn, o_z, o_g = out_refs[:6]
    i = pl.program_id(1)
    n = pl.num_programs(1)
    tm = x_ref.shape[1]
    has_prev = (i > 0).astype(F32)
    has_next = (i < n - 1).astype(F32)
    x = x_ref[0]
    xh = jnp.concatenate([xp_ref[0], xn_ref[0]], axis=0)
    if apply_ln:
        x = _layer_norm(x, lng_ref[...], lnb_ref[...])
        xh = _layer_norm(xh, lng_ref[...], lnb_ref[...])
        out_refs[6][0] = x
    h = (x * (1.0 + sc_ref[0]) + sh_ref[0]).astype(BF16)
    hh = (xh * (1.0 + sc_ref[0]) + sh_ref[0]).astype(BF16)
    gates = _dot(h, wg_ref[...])
    gp = gp_ref[...]
    glog = -gp[0:1, :] * _softplus(gates + gp[1:2, :])
    beta = _sigmoid(gates)

    ph_d = _dot(hh, w_ref[:, OFF_DN:OFF_DN + 3 * DN_W])
    ph_a = _dot(hh, w_ref[:, CONV_CH:OFF_ATT])
    pd = _dot(h, w_ref[:, OFF_DN:OFF_DN + 3 * DN_W])

    lane = _iota2((CUM_ROWS, LANES), 1)
    for r0 in range(0, tm, CUM_ROWS):
        g = glog[r0:r0 + CUM_ROWS]
        pre = _dot_exact_rhs(cum_ref[...], g)
        tot = jnp.concatenate([jnp.broadcast_to(pre[c + CHUNK - 1:c + CHUNK], (CHUNK, LANES))
                               for c in range(0, CUM_ROWS, CHUNK)], axis=0)
        o_g[0, r0:r0 + CUM_ROWS, :] = jnp.where(
            lane < DN_HEADS, pre, jnp.where(lane < 2 * DN_HEADS, tot - pre + g, beta[r0:r0 + CUM_ROWS]))

    dn_ext[0:F32_SUBLANES, :] = ph_d[:F32_SUBLANES] * has_prev
    dn_ext[F32_SUBLANES:F32_SUBLANES + tm, :] = pd
    dn_ext[F32_SUBLANES + tm:, :] = ph_d[F32_SUBLANES:] * has_next
    ones_bd = bdb_ref[...]
    half = tm // 2

    def dn_post(r0):
        mid = dn_ext[F32_SUBLANES + r0:F32_SUBLANES + r0 + half, :]
        qkv = _silu(_conv3(dn_ext.at[pl.ds(r0, half + 2 * F32_SUBLANES)], mid, cwd_ref[...], half))
        q = qkv[:, :DN_W]
        k = qkv[:, DN_W:2 * DN_W]
        rows = pl.ds(r0, half)
        o_dn[0, rows, :DN_W] = (q * (lax.rsqrt(_head_sumsq(q, ones_bd) + RMS_EPS) * (DN_DIM ** -0.5))).astype(BF16)
        o_dn[0, rows, DN_W:2 * DN_W] = (k * lax.rsqrt(_head_sumsq(k, ones_bd) + RMS_EPS)).astype(BF16)
        o_dn[0, rows, 2 * DN_W:] = qkv[:, 2 * DN_W:].astype(BF16)

    dn_post(0)
    pa = _dot(h, w_ref[:, :OFF_ATT])
    dn_post(half)
    o_q[0] = (_dot(h, w_ref[:, OFF_ATT:OFF_ATT + ATT_Q]) * (HEAD_DIM ** -0.5 * LOG2E)).astype(BF16)

    cu = pa[:, CONV_CH:2 * CONV_CH] * pa[:, 2 * CONV_CH:]
    cuh = ph_a[:, :CONV_CH] * ph_a[:, CONV_CH:]
    cu_ext[0:F32_SUBLANES, :] = cuh[:F32_SUBLANES] * has_prev
    cu_ext[F32_SUBLANES:F32_SUBLANES + tm, :] = cu
    cu_ext[F32_SUBLANES + tm:, :] = cuh[F32_SUBLANES:] * has_next
    o_ya[0] = (pa[:, :CONV_CH] * _conv3(cu_ext, cu, cwa_ref[...], tm)).astype(BF16)

    o_kv[0] = _dot(h, w_ref[:, OFF_ATT + ATT_Q:OFF_DN]).astype(BF16)
    o_z[0] = _dot(h, w_ref[:, OFF_DN + 3 * DN_W:OFF_GATE]).astype(BF16)


def _block_diag_ones():
    r = np.arange(DN_W)
    return jnp.asarray((r[:, None] // DN_DIM) == (r[None, :] // DN_DIM), BF16)


def _chunk_cumsum_selector():
    t = np.arange(CUM_ROWS)
    same_chunk = (t[:, None] // CHUNK) == (t[None, :] // CHUNK)
    return jnp.asarray(same_chunk & (t[None, :] <= t[:, None]), BF16)


def _in_projection(x, mod_l, b0, ln_g, ln_b, w_main, w_gate, conv_a_w, dn_conv_w, gate_params, apply_ln):
    bsz, s, _ = x.shape
    tm = min(ROW_TILE, s)
    assert s % tm == 0 and tm % CUM_ROWS == 0, (s, tm)
    nh = tm // F32_SUBLANES
    row = lambda b, i: (b, i, 0)
    full2 = lambda b, i: (0, 0)
    const = lambda shape: pl.BlockSpec(shape, full2, pipeline_mode=pl.Buffered(1))
    widths = (CONV_CH, ATT_Q, 2 * ATT_KV, 3 * DN_W, DN_W, LANES)
    dtypes = (BF16, BF16, BF16, BF16, BF16, F32)
    out_shape = [jax.ShapeDtypeStruct((bsz, s, w), dt) for w, dt in zip(widths, dtypes)]
    out_specs = [pl.BlockSpec((1, tm, w), row) for w in widths]
    if apply_ln:
        out_shape.append(jax.ShapeDtypeStruct((bsz, s, D_MODEL), F32))
        out_specs.append(pl.BlockSpec((1, tm, D_MODEL), row))
    return pl.pallas_call(
        functools.partial(_inproj_kernel, apply_ln),
        grid=(bsz, s // tm),
        in_specs=[
            pl.BlockSpec((1, tm, D_MODEL), row),
            pl.BlockSpec((1, F32_SUBLANES, D_MODEL), lambda b, i: (b, jnp.maximum(i * nh - 1, 0), 0)),
            pl.BlockSpec((1, F32_SUBLANES, D_MODEL),
                         lambda b, i: (b, jnp.minimum((i + 1) * nh, s // F32_SUBLANES - 1), 0)),
            pl.BlockSpec((1, 1, D_MODEL), lambda b, i: (b0 + b, 0, 0)),
            pl.BlockSpec((1, 1, D_MODEL), lambda b, i: (b0 + b, 0, 1)),
            pl.BlockSpec((1, D_MODEL), full2),
            pl.BlockSpec((1, D_MODEL), full2),
            const((D_MODEL, OFF_GATE)),
            const((D_MODEL, LANES)),
            const((F32_SUBLANES, CONV_CH)),
            const((F32_SUBLANES, 3 * DN_W)),
            const((DN_W, DN_W)),
            const((F32_SUBLANES, LANES)),
            const((CUM_ROWS, CUM_ROWS)),
        ],
        out_specs=out_specs,
        out_shape=out_shape,
        scratch_shapes=[pltpu.VMEM((tm + 2 * F32_SUBLANES, CONV_CH), F32),
                        pltpu.VMEM((tm + 2 * F32_SUBLANES, 3 * DN_W), F32)],
        compiler_params=pltpu.CompilerParams(dimension_semantics=("parallel", "parallel"),
                                             vmem_limit_bytes=VMEM_LIMIT_BYTES),
        name="in_projection",
    )(x, x, x, mod_l, mod_l, ln_g, ln_b, w_main, w_gate, conv_a_w, dn_conv_w, _block_diag_ones(), gate_params,
      _chunk_cumsum_selector())


def _attention_bias():
    qi = jnp.arange(ATT_BLOCK)[:, None]
    ki = jnp.arange(3 * ATT_BLOCK)[None, :]
    dist = jnp.abs(ki - ATT_BLOCK - qi)
    slopes = jnp.exp2(-8.0 * jnp.arange(1, N_HEADS + 1, dtype=F32) / N_HEADS)
    base = -(slopes * LOG2E)[:, None, None] * dist.astype(F32)[None]
    variants = []
    for var in range(4):
        valid = dist <= WINDOW
        if var & 1:
            valid = valid & (ki >= ATT_BLOCK)
        if var & 2:
            valid = valid & (ki < 2 * ATT_BLOCK)
        variants.append(jnp.where(valid[None], base, -jnp.inf))
    return jnp.stack(variants)


def _attention_kernel(q_ref, kv_ref, kv_prev_ref, kv_next_ref, bias_ref, sink_ref, o_ref,
                      ka_ref, kb_ref, va_ref, vb_ref):
    i = pl.program_id(1)
    n = pl.num_programs(1)
    tq = q_ref.shape[1]

    def fill(r0, blk):
        kk = blk[:, :LANES].astype(F32)
        vv = blk[:, LANES:].astype(F32)
        lo = _iota2(kk.shape, 1) < HEAD_DIM
        kks = pltpu.roll(kk, HEAD_DIM, 1)
        vvs = pltpu.roll(vv, HEAD_DIM, 1)
        rows = pl.ds(r0, blk.shape[0])
        ka_ref[0, rows, :] = jnp.where(lo, kk, 0.0).astype(BF16)
        kb_ref[0, rows, :] = jnp.where(lo, 0.0, kks).astype(BF16)
        ka_ref[1, rows, :] = jnp.where(lo, kks, 0.0).astype(BF16)
        kb_ref[1, rows, :] = jnp.where(lo, 0.0, kk).astype(BF16)
        va_ref[0, rows, :] = jnp.where(lo, vv, 0.0).astype(BF16)
        vb_ref[0, rows, :] = jnp.where(lo, 0.0, vvs).astype(BF16)
        va_ref[1, rows, :] = jnp.where(lo, vvs, 0.0).astype(BF16)
        vb_ref[1, rows, :] = jnp.where(lo, 0.0, vv).astype(BF16)

    fill(0, kv_prev_ref[0])
    fill(ATT_BLOCK, kv_ref[0])
    fill(ATT_BLOCK + tq, kv_next_ref[0])

    nwin = 3 * ATT_BLOCK
    nsub = tq // ATT_BLOCK
    lane_lo = _iota2((ATT_BLOCK, LANES), 1) < HEAD_DIM
    grp = N_HEADS // N_KV_HEADS

    heads = range(N_HEADS)
    for j in range(nsub):
        blk = i * nsub + j
        var = jnp.where(blk == 0, 1, 0) + jnp.where(blk == n * nsub - 1, 2, 0)
        win = pl.ds(j * ATT_BLOCK, nwin)
        qrows = pl.ds(j * ATT_BLOCK, ATT_BLOCK)
        kz = [(kb_ref if h % 2 else ka_ref)[h // grp, win, :] for h in heads]
        vz = [(vb_ref if h % 2 else va_ref)[h // grp, win, :] for h in heads]
        qp = [q_ref[0, qrows, (h // 2) * LANES:(h // 2 + 1) * LANES] for h in heads]
        sc = [_dot_nt(qp[h], kz[h]) + bias_ref[var, h] for h in heads]
        m = [jnp.maximum(jnp.max(sc[h], axis=-1, keepdims=True), sink_ref[h]) for h in heads]
        e = [jnp.exp2(sc[h] - m[h]) for h in heads]
        rcp = [1.0 / (jnp.sum(e[h], axis=-1, keepdims=True) + jnp.exp2(sink_ref[h] - m[h])) for h in heads]
        pv = [_dot(e[h].astype(BF16), vz[h]) for h in heads]
        for pair in range(N_HEADS // 2):
            out = (pv[2 * pair] + pv[2 * pair + 1]) * jnp.where(lane_lo, rcp[2 * pair], rcp[2 * pair + 1])
            o_ref[0, qrows, pair * LANES:(pair + 1) * LANES] = out.astype(BF16)


def _attention(q, kv, bias, sink_log2):
    bsz, s, _ = q.shape
    tq = min(ATT_TILE, s)
    nb = tq // ATT_BLOCK
    row = lambda b, i: (b, i, 0)
    return pl.pallas_call(
        _attention_kernel,
        grid=(bsz, s // tq),
        in_specs=[
            pl.BlockSpec((1, tq, ATT_Q), row),
            pl.BlockSpec((1, tq, 2 * ATT_KV), row),
            pl.BlockSpec((1, ATT_BLOCK, 2 * ATT_KV), lambda b, i: (b, jnp.maximum(i * nb - 1, 0), 0)),
            pl.BlockSpec((1, ATT_BLOCK, 2 * ATT_KV),
                         lambda b, i: (b, jnp.minimum((i + 1) * nb, s // ATT_BLOCK - 1), 0)),
            pl.BlockSpec(bias.shape, lambda b, i: (0, 0, 0, 0), pipeline_mode=pl.Buffered(1)),
            pl.BlockSpec(memory_space=pltpu.SMEM),
        ],
        out_specs=pl.BlockSpec((1, tq, ATT_Q), row),
        out_shape=jax.ShapeDtypeStruct((bsz, s, ATT_Q), BF16),
        scratch_shapes=[
            pltpu.VMEM((N_KV_HEADS, tq + 2 * ATT_BLOCK, LANES), BF16),
            pltpu.VMEM((N_KV_HEADS, tq + 2 * ATT_BLOCK, LANES), BF16),
            pltpu.VMEM((N_KV_HEADS, tq + 2 * ATT_BLOCK, LANES), BF16),
            pltpu.VMEM((N_KV_HEADS, tq + 2 * ATT_BLOCK, LANES), BF16),
        ],
        compiler_params=pltpu.CompilerParams(dimension_semantics=("parallel", "parallel"),
                                             vmem_limit_bytes=VMEM_LIMIT_BYTES),
        name="window_attention",
    )(q, kv, kv, kv, bias, sink_log2)


N_LEVELS = 6
CM_EYE, CM_NEG, CM_LEVEL0 = 0, 1, 2


def _dn_constants(reverse, ts):
    i = np.arange(CHUNK)[:, None]
    j = (np.arange(DN_W) % DN_DIM)[None, :]
    tri_incl = (i <= j) if reverse else (i >= j)
    rows = [(i == j).astype(np.float32), np.where(tri_incl, 0.0, -np.inf).astype(np.float32)]
    for lvl in range(1, N_LEVELS + 1):
        half = 1 << (lvl - 1)
        same_blk = (i >> lvl) == (j >> lvl)
        i_hi, j_hi = (i & half) != 0, (j & half) != 0
        rows.append((same_blk & ~i_hi & j_hi if reverse else same_blk & i_hi & ~j_hi).astype(np.float32))
    chunk_masks = np.stack(rows)
    r = np.arange(DN_W)
    mask_bd = (r[:, None] // DN_DIM) == (r[None, :] // DN_DIM)
    bd_f32 = np.stack([mask_bd.astype(np.float32), np.eye(DN_W, dtype=np.float32)])
    goff, boff = (DN_HEADS, 3 * DN_HEADS) if reverse else (0, 2 * DN_HEADS)
    er = np.arange(LANES)[:, None]
    ec = (np.arange(DN_W) // DN_DIM)[None, :]
    expand = np.stack([er == ec + goff, er == ec + boff])
    return (jnp.asarray(chunk_masks), jnp.asarray(bd_f32), jnp.asarray(mask_bd, BF16),
            jnp.asarray(expand, BF16))


def _bd(y, bdb_ref):
    return jnp.concatenate([y] * DN_HEADS, axis=0) * bdb_ref[...]


def _dn_prep(reverse, q, k, v, gfeat, cm_ref, bdf_ref, bdb_ref, exp_ref):
    nchunk = q.shape[0] // CHUNK
    cs = range(nchunk)
    rows = [slice(c * CHUNK, (c + 1) * CHUNK) for c in cs]
    bd = functools.partial(_bd, bdb_ref=bdb_ref)

    kb16 = [k[r].astype(BF16) for r in rows]
    gq = [_dot_nt(jnp.concatenate([kb, q[r].astype(BF16)], axis=0), bd(kb)) for kb, r in zip(kb16, rows)]
    yield
    gcol = _dot_exact_lhs(gfeat, exp_ref[0])
    bcol = _dot_exact_lhs(gfeat, exp_ref[1])
    yield
    eye_p = cm_ref[CM_EYE]
    egc = jnp.exp(gcol)
    kbg = k * (bcol * egc)
    vb = v * bcol
    qe = q * egc
    gcs = [gcol[r] for r in rows]
    grow = [jnp.sum(eye_p * g, axis=0, keepdims=True) for g in gcs]
    gtot = [g[0:1, :] if reverse else g[CHUNK - 1:CHUNK, :] for g in gcs]
    decay = [jnp.exp(g - gr + cm_ref[CM_NEG]) for g, gr in zip(gcs, grow)]
    lb = [g[:CHUNK] * bcol[r] * d for g, r, d in zip(gq, rows, decay)]
    ab = [(g[CHUNK:] * d).astype(BF16) for g, d in zip(gq, decay)]
    x = [eye_p - cm_ref[CM_LEVEL0] * l for l in lb]
    for lvl in range(1, N_LEVELS):
        xb = [xc.astype(BF16) for xc in x]
        z = [_dot((cm_ref[CM_LEVEL0 + lvl] * l).astype(BF16), bd(b)) for l, b in zip(lb, xb)]
        yield
        x = [xc - _dot(b, bd(zc.astype(BF16))) for xc, b, zc in zip(x, xb, z)]
        yield
    xb = [xc.astype(BF16) for xc in x]
    u = [_dot(b, bd(vb[r].astype(BF16))) for b, r in zip(xb, rows)]
    yield
    w = [_dot(b, bd(kbg[r].astype(BF16))) for b, r in zip(xb, rows)]
    yield
    p = [(qe[r] - _dot(a, bd(wc.astype(BF16)))).astype(BF16) for r, a, wc in zip(rows, ab, w)]
    yield
    rr = [_dot(a, bd(uc.astype(BF16))) for a, uc in zip(ab, u)]
    yield
    kd = [(k[r] * jnp.exp(gt - g)).astype(BF16) for r, gt, g in zip(rows, gtot, gcs)]
    full = [_dot_tn(jnp.concatenate([wc, uc], axis=1).astype(BF16), kdc) for wc, uc, kdc in zip(w, u, kd)]
    yield
    bdm = [(bdf_ref[1] * jnp.exp(gt) - bdf_ref[0] * f[:DN_W]).astype(BF16) for gt, f in zip(gtot, full)]
    nfull = [bdf_ref[0] * f[DN_W:] for f in full]
    nt = [nf[0:64] + nf[64:128] + nf[128:192] + nf[192:256] for nf in nfull]
    return p, rr, bdm, nt


def _dn_kernel(reverse, n_tiles, x_ref, g_ref, cm_ref, bdf_ref, bdb_ref, exp_ref, *rest):
    if reverse:
        z_ref, of_ref, ng_ref, o_ref, st_ref, p_scr, r_scr, m_scr, n_scr = rest
    else:
        o_ref, st_ref, p_scr, r_scr, m_scr, n_scr = rest
    s = pl.program_id(0)
    nchunk = x_ref.shape[1] // CHUNK

    @pl.when(s == 0)
    def _():
        st_ref[...] = jnp.zeros(st_ref.shape, F32)
        p_scr[1] = jnp.zeros(p_scr.shape[1:], BF16)
        r_scr[1] = jnp.zeros(r_scr.shape[1:], F32)
        m_scr[1] = jnp.zeros(m_scr.shape[1:], BF16)
        n_scr[1] = jnp.zeros(n_scr.shape[1:], F32)

    cur = lax.rem(s, 2)
    prev = 1 - cur
    scanned = jnp.maximum(s - 1, 0)
    first = lax.rem(scanned, n_tiles) == 0
    st = jnp.where(first, 0.0, st_ref[...])

    qkv = x_ref[0].astype(F32)
    prep = _dn_prep(reverse, qkv[:, :DN_W], qkv[:, DN_W:2 * DN_W], qkv[:, 2 * DN_W:], g_ref[0],
                    cm_ref, bdf_ref, bdb_ref, exp_ref)

    def scan_step(c, st):
        rows = pl.ds(c * CHUNK, CHUNK)
        sb = st.astype(BF16)
        st_new = _dot(sb, m_scr[prev, c]) + n_scr[prev, rows, :]
        o = _dot_nt(p_scr[prev, rows, :], _bd(sb, bdb_ref)) + r_scr[prev, rows, :]
        return st_new, (rows, o)

    def finish(pending):
        rows, o = pending
        if reverse:
            o = o + of_ref[0, rows, :]
            ms = _head_sumsq(o, bdb_ref[...]) * (1.0 / DN_DIM)
            y = o * lax.rsqrt(ms + RMS_EPS) * ng_ref[...] * _silu(z_ref[0, rows, :].astype(F32))
            o_ref[0, rows, :] = y.astype(BF16)
        else:
            o_ref[0, rows, :] = o

    order = list(reversed(range(nchunk))) if reverse else list(range(nchunk))
    groups = 0
    pending = None
    while True:
        try:
            next(prep)
        except StopIteration as done:
            p, rr, bdm, nt = done.value
            break
        groups += 1
        if groups % 2 == 0 and order:
            if pending is not None:
                finish(pending)
            st, pending = scan_step(order.pop(0), st)
    while order:
        finish(pending)
        st, pending = scan_step(order.pop(0), st)
    finish(pending)
    st_ref[...] = st
    for c in range(nchunk):
        rows = pl.ds(c * CHUNK, CHUNK)
        p_scr[cur, rows, :] = p[c]
        r_scr[cur, rows, :] = rr[c]
        m_scr[cur, c] = bdm[c]
        n_scr[cur, rows, :] = nt[c]


def _delta_pass(reverse, dnqkv, gfeat, z=None, o_fwd=None, norm_g=None):
    bsz, s, _ = dnqkv.shape
    ts = min(DN_TILE, s)
    n = s // ts
    total = bsz * n
    pos = (lambda i: n - 1 - i) if reverse else (lambda i: i)

    def tile_of(flat):
        return flat // n, pos(lax.rem(flat, n)), 0

    prepared = lambda s_: tile_of(jnp.minimum(s_, total - 1))
    scanned = lambda s_: tile_of(jnp.maximum(s_ - 1, 0))
    in_specs = [
        pl.BlockSpec((1, ts, 3 * DN_W), prepared),
        pl.BlockSpec((1, ts, LANES), prepared),
    ]
    consts = _dn_constants(reverse, ts)
    in_specs += [pl.BlockSpec(c.shape, (lambda nd: lambda s_: (0,) * nd)(c.ndim), pipeline_mode=pl.Buffered(1))
                 for c in consts]
    args = [dnqkv, gfeat, *consts]
    if reverse:
        in_specs += [pl.BlockSpec((1, ts, DN_W), scanned), pl.BlockSpec((1, ts, DN_W), scanned),
                     pl.BlockSpec((1, DN_W), lambda s_: (0, 0))]
        args += [z, o_fwd, norm_g]
        out_dtype = BF16
    else:
        out_dtype = F32
    return pl.pallas_call(
        functools.partial(_dn_kernel, reverse, n),
        grid=(total + 1,),
        in_specs=in_specs,
        out_specs=pl.BlockSpec((1, ts, DN_W), scanned),
        out_shape=jax.ShapeDtypeStruct((bsz, s, DN_W), out_dtype),
        scratch_shapes=[pltpu.VMEM((DN_DIM, DN_W), F32),
                        pltpu.VMEM((2, ts, DN_W), BF16),
                        pltpu.VMEM((2, ts, DN_W), F32),
                        pltpu.VMEM((2, ts // CHUNK, DN_W, DN_W), BF16),
                        pltpu.VMEM((2, ts, DN_W), F32)],
        compiler_params=pltpu.CompilerParams(dimension_semantics=("arbitrary",),
                                             vmem_limit_bytes=VMEM_LIMIT_BYTES),
        name="delta_bwd" if reverse else "delta_fwd",
    )(*args)


def _out_mlp_kernel(x_ref, ya_ref, yb_ref, yc_ref, g1_ref, sh2_ref, sc2_ref, g2_ref, wo_a_ref, wo_b_ref,
                    wo_c_ref, ln1g_ref, ln1b_ref, w1_ref, b1_ref, w2_ref, b2_ref, ln2g_ref, ln2b_ref, o_ref):
    tm = x_ref.shape[1]
    groups = [pl.ds(g * (tm // MLP_ROW_GROUPS), tm // MLP_ROW_GROUPS) for g in range(MLP_ROW_GROUPS)]
    y = [_dot(ya_ref[0, r, :], wo_a_ref[...]) + _dot(yb_ref[0, r, :], wo_b_ref[...])
         + _dot(yc_ref[0, r, :], wo_c_ref[...]) for r in groups]
    x1 = [_layer_norm(ALPHA * x_ref[0, r, :] + (1.0 + g1_ref[0]) * yg, ln1g_ref[...], ln1b_ref[...])
          for r, yg in zip(groups, y)]
    h = [(xg * (1.0 + sc2_ref[0]) + sh2_ref[0]).astype(BF16) for xg in x1]
    f = [None] * MLP_ROW_GROUPS
    for c in range(D_FF // FF_CHUNK):
        cols = slice(c * FF_CHUNK, (c + 1) * FF_CHUNK)
        a = [jnp.maximum(_dot(hg, w1_ref[:, cols]) + b1_ref[:, cols], 0.0) for hg in h]
        part = [_dot((ag * ag).astype(BF16), w2_ref[cols, :]) for ag in a]
        f = [pg if fg is None else fg + pg for fg, pg in zip(f, part)]
    for r, xg, fg in zip(groups, x1, f):
        o_ref[0, r, :] = _layer_norm(ALPHA * xg + (1.0 + g2_ref[0]) * (fg + b2_ref[...]),
                                     ln2g_ref[...], ln2b_ref[...])


def _out_mlp(x, ya, yb, yc, mod_l, b0, wo_a, wo_b, wo_c, ln1g, ln1b, w1, b1, w2, b2, ln2g, ln2b):
    bsz, s, _ = x.shape
    tm = min(ROW_TILE, s)
    row = lambda b, i: (b, i, 0)
    full2 = lambda b, i: (0, 0)
    modspec = lambda k: pl.BlockSpec((1, 1, D_MODEL), lambda b, i: (b0 + b, 0, k))
    const = lambda shape: pl.BlockSpec(shape, full2, pipeline_mode=pl.Buffered(1))
    return pl.pallas_call(
        _out_mlp_kernel,
        grid=(bsz, s // tm),
        in_specs=[
            pl.BlockSpec((1, tm, D_MODEL), row),
            pl.BlockSpec((1, tm, CONV_CH), row),
            pl.BlockSpec((1, tm, ATT_Q), row),
            pl.BlockSpec((1, tm, DN_W), row),
            modspec(2), modspec(3), modspec(4), modspec(5),
            const((CONV_CH, D_MODEL)), const((ATT_Q, D_MODEL)), const((DN_W, D_MODEL)),
            const((1, D_MODEL)), const((1, D_MODEL)),
            const((D_MODEL, D_FF)), const((1, D_FF)),
            const((D_FF, D_MODEL)), const((1, D_MODEL)),
            const((1, D_MODEL)), const((1, D_MODEL)),
        ],
        out_specs=pl.BlockSpec((1, tm, D_MODEL), row),
        out_shape=jax.ShapeDtypeStruct((bsz, s, D_MODEL), F32),
        compiler_params=pltpu.CompilerParams(dimension_semantics=("parallel", "parallel"),
                                             vmem_limit_bytes=VMEM_LIMIT_BYTES),
        name="out_mlp",
    )(x, ya, yb, yc, mod_l, mod_l, mod_l, mod_l, wo_a, wo_b, wo_c, ln1g, ln1b, w1, b1, w2, b2, ln2g, ln2b)


def _pad_rows(a, rows):
    return jnp.concatenate([a, jnp.zeros((rows - a.shape[0],) + a.shape[1:], a.dtype)], axis=0)


def _trunk(x, mod, b0, prm):
    for l in range(DEPTH):
        mod_l = mod[l]
        outs = _in_projection(x, mod_l, b0, prm["ln_in_g"], prm["ln_in_b"], prm["w_main"][l], prm["w_gate"][l],
                              prm["conv_a_w"][l], prm["dn_conv_w"][l], prm["gate_params"][l], apply_ln=(l == 0))
        ya, q, kv, dnqkv, z, gfeat = outs[:6]
        if l == 0:
            x = outs[6]
        yb = _attention(q, kv, prm["attn_bias"], prm["sink_log2"][l])
        o_fwd = _delta_pass(False, dnqkv, gfeat)
        yc = _delta_pass(True, dnqkv, gfeat, z=z, o_fwd=o_fwd, norm_g=prm["dn_norm_g"][l])
        x = _out_mlp(x, ya, yb, yc, mod_l, b0, prm["wo_a"][l], prm["wo_b"][l], prm["wo_c"][l],
                     prm["ln1_g"][l], prm["ln1_b"][l], prm["w1"][l], prm["b1"][l], prm["w2"][l], prm["b2"][l],
                     prm["ln2_g"][l], prm["ln2_b"][l])
    return x


def kernel(x_prompt, x_sample, c_prompt, c_sample, ln_in_g, ln_in_b, w_mod, b_mod, w_in, conv_a_w, attn_sink,
           dn_conv_w, dn_a_log_f, dn_a_log_b, dn_dt_bias_f, dn_dt_bias_b, dn_norm_g, w_out, ln1_g, ln1_b,
           w1, b1, w2, b2, ln2_g, ln2_b):
    nb_p, nb_s = c_prompt.shape[0], c_sample.shape[0]
    bp = -(-(nb_p + nb_s) // BF16_SUBLANES) * BF16_SUBLANES
    c_all = _pad_rows(jnp.concatenate([c_prompt, c_sample], axis=0), bp)
    mod = _modulation(c_all, w_mod, b_mod)
    mod = mod.reshape(DEPTH, bp, 1, 6 * D_MODEL)

    row = lambda a: a.reshape(DEPTH, 1, -1)
    zeros4 = jnp.zeros((DEPTH, DN_HEADS), F32)
    gp0 = jnp.concatenate([jnp.exp(dn_a_log_f), jnp.exp(dn_a_log_b), zeros4, zeros4], axis=1)
    gp1 = jnp.concatenate([dn_dt_bias_f, dn_dt_bias_b, zeros4, zeros4], axis=1)
    gate_params = jnp.stack([gp0, gp1], axis=1)
    gate_params = jnp.pad(gate_params, ((0, 0), (0, F32_SUBLANES - 2), (0, LANES - 4 * DN_HEADS)))
    pad_taps = lambda w: jnp.pad(w, ((0, 0), (0, F32_SUBLANES - w.shape[1]), (0, 0)))
    prm = dict(
        ln_in_g=ln_in_g.reshape(1, -1), ln_in_b=ln_in_b.reshape(1, -1),
        w_main=w_in[:, :, :OFF_GATE].astype(BF16),
        w_gate=jnp.pad(w_in[:, :, OFF_GATE:], ((0, 0), (0, 0), (0, LANES - 4 * DN_HEADS))).astype(BF16),
        conv_a_w=pad_taps(conv_a_w), attn_bias=_attention_bias(), sink_log2=attn_sink * LOG2E,
        dn_conv_w=pad_taps(dn_conv_w), gate_params=gate_params,
        dn_norm_g=jnp.tile(dn_norm_g, (1, DN_HEADS)).reshape(DEPTH, 1, DN_W),
        wo_a=w_out[:, :CONV_CH, :].astype(BF16), wo_b=w_out[:, CONV_CH:CONV_CH + ATT_Q, :].astype(BF16),
        wo_c=w_out[:, CONV_CH + ATT_Q:, :].astype(BF16),
        ln1_g=row(ln1_g), ln1_b=row(ln1_b), w1=w1.astype(BF16), b1=row(b1), w2=w2.astype(BF16), b2=row(b2),
        ln2_g=row(ln2_g), ln2_b=row(ln2_b),
    )
    y_prompt = _trunk(x_prompt, mod, 0, prm)
    y_sample = _trunk(x_sample, mod, nb_p, prm)
    return (y_prompt, y_sample)
```

```python
import functools

import numpy as np
import jax
import jax.numpy as jnp
from jax import lax
from jax.experimental import pallas as pl
from jax.experimental.pallas import tpu as pltpu

F32 = jnp.float32
BF16 = jnp.bfloat16

D_MODEL = 1024
DEPTH = 4
CONV_CH = 256
N_HEADS = 8
N_KV_HEADS = 2
HEAD_DIM = 64
WINDOW = 128
DN_HEADS = 4
DN_DIM = 64
DN_W = DN_HEADS * DN_DIM
D_FF = 4 * D_MODEL
ATT_Q = N_HEADS * HEAD_DIM
ATT_KV = N_KV_HEADS * HEAD_DIM
OFF_ATT = 3 * CONV_CH
OFF_DN = OFF_ATT + ATT_Q + 2 * ATT_KV
OFF_GATE = OFF_DN + 4 * DN_W
D_IN = OFF_GATE + 4 * DN_HEADS
ALPHA = (2.0 * DEPTH) ** 0.25
LN_EPS = 1e-5
RMS_EPS = 1e-6
LOG2E = 1.4426950408889634

LANES = 128
BF16_SUBLANES = 16
F32_SUBLANES = 8
VMEM_LIMIT_BYTES = 56 * 1024 * 1024

ROW_TILE = 512
INPROJ_TILE = 1024
ATT_TILE = 1024
ATT_BLOCK = 128
DN_TILE = 512
CHUNK = 64
CUM_ROWS = 256
FF_CHUNK = 1024
MLP_ROW_GROUPS = 2


def _dot(a, b):
    return jnp.dot(a, b, preferred_element_type=F32)


def _dot_nt(a, b):
    return lax.dot_general(a, b, (((1,), (1,)), ((), ())), preferred_element_type=F32)


def _dot_tn(a, b):
    return lax.dot_general(a, b, (((0,), (0,)), ((), ())), preferred_element_type=F32)


def _sigmoid(x):
    return 1.0 / (1.0 + jnp.exp(-x))


def _silu(x):
    return x * _sigmoid(x)


def _softplus(x):
    return jnp.maximum(x, 0.0) + jnp.log(1.0 + jnp.exp(-jnp.abs(x)))


def _layer_norm(x, g, b):
    mu = jnp.mean(x, axis=-1, keepdims=True)
    xc = x - mu
    var = jnp.mean(xc * xc, axis=-1, keepdims=True)
    return xc * lax.rsqrt(var + LN_EPS) * g + b


def _split3(x):
    hi = x.astype(BF16)
    r1 = x - hi.astype(F32)
    mid = r1.astype(BF16)
    lo = (r1 - mid.astype(F32)).astype(BF16)
    return hi, mid, lo


def _dot_exact_rhs(sel, x):
    hi, mid, lo = _split3(x)
    return _dot(sel, hi) + _dot(sel, mid) + _dot(sel, lo)


def _dot_exact_lhs(x, sel):
    hi, mid, lo = _split3(x)
    return _dot(hi, sel) + _dot(mid, sel) + _dot(lo, sel)


def _iota2(shape, axis):
    return lax.broadcasted_iota(jnp.int32, shape, axis)


def _mod_kernel(c_ref, w_ref, b_ref, o_ref):
    s = _silu(c_ref[...]).astype(BF16)
    o_ref[0] = _dot(s, w_ref[0].astype(BF16)) + b_ref[0]


def _modulation(c_all, w_mod, b_mod):
    bp = c_all.shape[0]
    nblk = w_mod.shape[2] // D_MODEL
    return pl.pallas_call(
        _mod_kernel,
        grid=(DEPTH, nblk),
        in_specs=[
            pl.BlockSpec((bp, D_MODEL), lambda l, k: (0, 0)),
            pl.BlockSpec((1, D_MODEL, D_MODEL), lambda l, k: (l, 0, k)),
            pl.BlockSpec((1, 1, D_MODEL), lambda l, k: (l, 0, k)),
        ],
        out_specs=pl.BlockSpec((1, bp, D_MODEL), lambda l, k: (l, 0, k)),
        out_shape=jax.ShapeDtypeStruct((DEPTH, bp, 6 * D_MODEL), F32),
        compiler_params=pltpu.CompilerParams(dimension_semantics=("arbitrary", "arbitrary"),
                                             vmem_limit_bytes=VMEM_LIMIT_BYTES),
        name="modulation",
    )(c_all, w_mod, b_mod.reshape(DEPTH, 1, 6 * D_MODEL))


def _conv3(ext, mid, taps, rows):
    return (taps[1:2, :] * mid + taps[0:1, :] * ext[F32_SUBLANES - 1:F32_SUBLANES - 1 + rows, :]
            + taps[2:3, :] * ext[F32_SUBLANES + 1:F32_SUBLANES + 1 + rows, :])


def _head_sumsq(y, ones_bd):
    y2 = y * y
    hi = y2.astype(BF16)
    lo = (y2 - hi.astype(F32)).astype(BF16)
    return _dot(hi, ones_bd) + _dot(lo, ones_bd)


def _inproj_kernel(apply_ln, x_ref, xp_ref, xn_ref, sh_ref, sc_ref, lng_ref, lnb_ref, w_ref, wg_ref,
                   cwa_ref, cwd_ref, bdb_ref, gp_ref, cum_ref, *rest):
    n_out = 7 if apply_ln else 6
    out_refs, (cu_ext, dn_ext) = rest[:n_out], rest[n_out:]
    o_ya, o_q, o_kv, o_dn, o_z, o_g = out_refs[:6]
    i = pl.program_id(1)
    n = pl.num_programs(1)
    tm = x_ref.shape[1]
    has_prev = (i > 0).astype(F32)
    has_next = (i < n - 1).astype(F32)
    x = x_ref[0]
    xh = jnp.concatenate([xp_ref[0], xn_ref[0]], axis=0)
    if apply_ln:
        x = _layer_norm(x, lng_ref[...], lnb_ref[...])
        xh = _layer_norm(xh, lng_ref[...], lnb_ref[...])
        out_refs[6][0] = x
    h = (x * (1.0 + sc_ref[0]) + sh_ref[0]).astype(BF16)
    hh = (xh * (1.0 + sc_ref[0]) + sh_ref[0]).astype(BF16)
    gates = _dot(h, wg_ref[...])
    gp = gp_ref[...]
    glog = -gp[0:1, :] * _softplus(gates + gp[1:2, :])
    beta = _sigmoid(gates)

    ph_d = _dot(hh, w_ref[:, OFF_DN:OFF_DN + 3 * DN_W])
    ph_a = _dot(hh, w_ref[:, CONV_CH:OFF_ATT])
    pd = _dot(h, w_ref[:, OFF_DN:OFF_DN + 3 * DN_W])

    lane = _iota2((CUM_ROWS, LANES), 1)
    for r0 in range(0, tm, CUM_ROWS):
        g = glog[r0:r0 + CUM_ROWS]
        pre = _dot_exact_rhs(cum_ref[...], g)
        tot = jnp.concatenate([jnp.broadcast_to(pre[c + CHUNK - 1:c + CHUNK], (CHUNK, LANES))
                               for c in range(0, CUM_ROWS, CHUNK)], axis=0)
        o_g[0, r0:r0 + CUM_ROWS, :] = jnp.where(
            lane < DN_HEADS, pre, jnp.where(lane < 2 * DN_HEADS, tot - pre + g, beta[r0:r0 + CUM_ROWS]))

    dn_ext[0:F32_SUBLANES, :] = ph_d[:F32_SUBLANES] * has_prev
    dn_ext[F32_SUBLANES:F32_SUBLANES + tm, :] = pd
    dn_ext[F32_SUBLANES + tm:, :] = ph_d[F32_SUBLANES:] * has_next
    ones_bd = bdb_ref[...]
    half = tm // 2

    def dn_post(r0):
        mid = dn_ext[F32_SUBLANES + r0:F32_SUBLANES + r0 + half, :]
        qkv = _silu(_conv3(dn_ext.at[pl.ds(r0, half + 2 * F32_SUBLANES)], mid, cwd_ref[...], half))
        q = qkv[:, :DN_W]
        k = qkv[:, DN_W:2 * DN_W]
        rows = pl.ds(r0, half)
        o_dn[0, rows, :DN_W] = (q * (lax.rsqrt(_head_sumsq(q, ones_bd) + RMS_EPS) * (DN_DIM ** -0.5))).astype(BF16)
        o_dn[0, rows, DN_W:2 * DN_W] = (k * lax.rsqrt(_head_sumsq(k, ones_bd) + RMS_EPS)).astype(BF16)
        o_dn[0, rows, 2 * DN_W:] = qkv[:, 2 * DN_W:].astype(BF16)

    dn_post(0)
    pa = _dot(h, w_ref[:, :OFF_ATT])
    dn_post(half)
    o_q[0] = (_dot(h, w_ref[:, OFF_ATT:OFF_ATT + ATT_Q]) * (HEAD_DIM ** -0.5 * LOG2E)).astype(BF16)

    cu = pa[:, CONV_CH:2 * CONV_CH] * pa[:, 2 * CONV_CH:]
    cuh = ph_a[:, :CONV_CH] * ph_a[:, CONV_CH:]
    cu_ext[0:F32_SUBLANES, :] = cuh[:F32_SUBLANES] * has_prev
    cu_ext[F32_SUBLANES:F32_SUBLANES + tm, :] = cu
    cu_ext[F32_SUBLANES + tm:, :] = cuh[F32_SUBLANES:] * has_next
    o_ya[0] = (pa[:, :CONV_CH] * _conv3(cu_ext, cu, cwa_ref[...], tm)).astype(BF16)

    o_kv[0] = _dot(h, w_ref[:, OFF_ATT + ATT_Q:OFF_DN]).astype(BF16)
    o_z[0] = _dot(h, w_ref[:, OFF_DN + 3 * DN_W:OFF_GATE]).astype(BF16)


def _block_diag_ones():
    r = np.arange(DN_W)
    return jnp.asarray((r[:, None] // DN_DIM) == (r[None, :] // DN_DIM), BF16)


def _chunk_cumsum_selector():
    t = np.arange(CUM_ROWS)
    same_chunk = (t[:, None] // CHUNK) == (t[None, :] // CHUNK)
    return jnp.asarray(same_chunk & (t[None, :] <= t[:, None]), BF16)


def _in_projection(x, mod_l, b0, ln_g, ln_b, w_main, w_gate, conv_a_w, dn_conv_w, gate_params, apply_ln):
    bsz, s, _ = x.shape
    tm = min(INPROJ_TILE, s)
    assert s % tm == 0 and tm % CUM_ROWS == 0, (s, tm)
    nh = tm // F32_SUBLANES
    row = lambda b, i: (b, i, 0)
    full2 = lambda b, i: (0, 0)
    const = lambda shape: pl.BlockSpec(shape, full2, pipeline_mode=pl.Buffered(1))
    widths = (CONV_CH, ATT_Q, 2 * ATT_KV, 3 * DN_W, DN_W, LANES)
    dtypes = (BF16, BF16, BF16, BF16, BF16, F32)
    out_shape = [jax.ShapeDtypeStruct((bsz, s, w), dt) for w, dt in zip(widths, dtypes)]
    out_specs = [pl.BlockSpec((1, tm, w), row) for w in widths]
    if apply_ln:
        out_shape.append(jax.ShapeDtypeStruct((bsz, s, D_MODEL), F32))
        out_specs.append(pl.BlockSpec((1, tm, D_MODEL), row))
    return pl.pallas_call(
        functools.partial(_inproj_kernel, apply_ln),
        grid=(bsz, s // tm),
        in_specs=[
            pl.BlockSpec((1, tm, D_MODEL), row),
            pl.BlockSpec((1, F32_SUBLANES, D_MODEL), lambda b, i: (b, jnp.maximum(i * nh - 1, 0), 0)),
            pl.BlockSpec((1, F32_SUBLANES, D_MODEL),
                         lambda b, i: (b, jnp.minimum((i + 1) * nh, s // F32_SUBLANES - 1), 0)),
            pl.BlockSpec((1, 1, D_MODEL), lambda b, i: (b0 + b, 0, 0)),
            pl.BlockSpec((1, 1, D_MODEL), lambda b, i: (b0 + b, 0, 1)),
            pl.BlockSpec((1, D_MODEL), full2),
            pl.BlockSpec((1, D_MODEL), full2),
            const((D_MODEL, OFF_GATE)),
            const((D_MODEL, LANES)),
            const((F32_SUBLANES, CONV_CH)),
            const((F32_SUBLANES, 3 * DN_W)),
            const((DN_W, DN_W)),
            const((F32_SUBLANES, LANES)),
            const((CUM_ROWS, CUM_ROWS)),
        ],
        out_specs=out_specs,
        out_shape=out_shape,
        scratch_shapes=[pltpu.VMEM((tm + 2 * F32_SUBLANES, CONV_CH), F32),
                        pltpu.VMEM((tm + 2 * F32_SUBLANES, 3 * DN_W), F32)],
        compiler_params=pltpu.CompilerParams(dimension_semantics=("parallel", "parallel"),
                                             vmem_limit_bytes=VMEM_LIMIT_BYTES),
        name="in_projection",
    )(x, x, x, mod_l, mod_l, ln_g, ln_b, w_main, w_gate, conv_a_w, dn_conv_w, _block_diag_ones(), gate_params,
      _chunk_cumsum_selector())


def _attention_bias():
    qi = jnp.arange(ATT_BLOCK)[:, None]
    ki = jnp.arange(3 * ATT_BLOCK)[None, :]
    dist = jnp.abs(ki - ATT_BLOCK - qi)
    slopes = jnp.exp2(-8.0 * jnp.arange(1, N_HEADS + 1, dtype=F32) / N_HEADS)
    base = -(slopes * LOG2E)[:, None, None] * dist.astype(F32)[None]
    variants = []
    for var in range(4):
        valid = dist <= WINDOW
        if var & 1:
            valid = valid & (ki >= ATT_BLOCK)
        if var & 2:
            valid = valid & (ki < 2 * ATT_BLOCK)
        variants.append(jnp.where(valid[None], base, -jnp.inf))
    return jnp.stack(variants)


def _attention_kernel(q_ref, kv_ref, kv_prev_ref, kv_next_ref, bias_ref, sink_ref, o_ref,
                      ka_ref, kb_ref, va_ref, vb_ref):
    i = pl.program_id(1)
    n = pl.num_programs(1)
    tq = q_ref.shape[1]

    def fill(r0, blk):
        kk = blk[:, :LANES].astype(F32)
        vv = blk[:, LANES:].astype(F32)
        lo = _iota2(kk.shape, 1) < HEAD_DIM
        kks = pltpu.roll(kk, HEAD_DIM, 1)
        vvs = pltpu.roll(vv, HEAD_DIM, 1)
        rows = pl.ds(r0, blk.shape[0])
        ka_ref[0, rows, :] = jnp.where(lo, kk, 0.0).astype(BF16)
        kb_ref[0, rows, :] = jnp.where(lo, 0.0, kks).astype(BF16)
        ka_ref[1, rows, :] = jnp.where(lo, kks, 0.0).astype(BF16)
        kb_ref[1, rows, :] = jnp.where(lo, 0.0, kk).astype(BF16)
        va_ref[0, rows, :] = jnp.where(lo, vv, 0.0).astype(BF16)
        vb_ref[0, rows, :] = jnp.where(lo, 0.0, vvs).astype(BF16)
        va_ref[1, rows, :] = jnp.where(lo, vvs, 0.0).astype(BF16)
        vb_ref[1, rows, :] = jnp.where(lo, 0.0, vv).astype(BF16)

    fill(0, kv_prev_ref[0])
    fill(ATT_BLOCK, kv_ref[0])
    fill(ATT_BLOCK + tq, kv_next_ref[0])

    nwin = 3 * ATT_BLOCK
    nsub = tq // ATT_BLOCK
    lane_lo = _iota2((ATT_BLOCK, LANES), 1) < HEAD_DIM
    grp = N_HEADS // N_KV_HEADS

    heads = range(N_HEADS)
    for j in range(nsub):
        blk = i * nsub + j
        var = jnp.where(blk == 0, 1, 0) + jnp.where(blk == n * nsub - 1, 2, 0)
        win = pl.ds(j * ATT_BLOCK, nwin)
        qrows = pl.ds(j * ATT_BLOCK, ATT_BLOCK)
        kz = [(kb_ref if h % 2 else ka_ref)[h // grp, win, :] for h in heads]
        vz = [(vb_ref if h % 2 else va_ref)[h // grp, win, :] for h in heads]
        qp = [q_ref[0, qrows, (h // 2) * LANES:(h // 2 + 1) * LANES] for h in heads]
        sc = [_dot_nt(qp[h], kz[h]) + bias_ref[var, h] for h in heads]
        m = [jnp.maximum(jnp.max(sc[h], axis=-1, keepdims=True), sink_ref[h]) for h in heads]
        e = [jnp.exp2(sc[h] - m[h]) for h in heads]
        rcp = [1.0 / (jnp.sum(e[h], axis=-1, keepdims=True) + jnp.exp2(sink_ref[h] - m[h])) for h in heads]
        pv = [_dot(e[h].astype(BF16), vz[h]) for h in heads]
        for pair in range(N_HEADS // 2):
            out = (pv[2 * pair] + pv[2 * pair + 1]) * jnp.where(lane_lo, rcp[2 * pair], rcp[2 * pair + 1])
            o_ref[0, qrows, pair * LANES:(pair + 1) * LANES] = out.astype(BF16)


def _attention(q, kv, bias, sink_log2):
    bsz, s, _ = q.shape
    tq = min(ATT_TILE, s)
    nb = tq // ATT_BLOCK
    row = lambda b, i: (b, i, 0)
    return pl.pallas_call(
        _attention_kernel,
        grid=(bsz, s // tq),
        in_specs=[
            pl.BlockSpec((1, tq, ATT_Q), row),
            pl.BlockSpec((1, tq, 2 * ATT_KV), row),
            pl.BlockSpec((1, ATT_BLOCK, 2 * ATT_KV), lambda b, i: (b, jnp.maximum(i * nb - 1, 0), 0)),
            pl.BlockSpec((1, ATT_BLOCK, 2 * ATT_KV),
                         lambda b, i: (b, jnp.minimum((i + 1) * nb, s // ATT_BLOCK - 1), 0)),
            pl.BlockSpec(bias.shape, lambda b, i: (0, 0, 0, 0), pipeline_mode=pl.Buffered(1)),
            pl.BlockSpec(memory_space=pltpu.SMEM),
        ],
        out_specs=pl.BlockSpec((1, tq, ATT_Q), row),
        out_shape=jax.ShapeDtypeStruct((bsz, s, ATT_Q), BF16),
        scratch_shapes=[
            pltpu.VMEM((N_KV_HEADS, tq + 2 * ATT_BLOCK, LANES), BF16),
            pltpu.VMEM((N_KV_HEADS, tq + 2 * ATT_BLOCK, LANES), BF16),
            pltpu.VMEM((N_KV_HEADS, tq + 2 * ATT_BLOCK, LANES), BF16),
            pltpu.VMEM((N_KV_HEADS, tq + 2 * ATT_BLOCK, LANES), BF16),
        ],
        compiler_params=pltpu.CompilerParams(dimension_semantics=("parallel", "parallel"),
                                             vmem_limit_bytes=VMEM_LIMIT_BYTES),
        name="window_attention",
    )(q, kv, kv, kv, bias, sink_log2)


N_LEVELS = 6
CM_EYE, CM_NEG, CM_LEVEL0 = 0, 1, 2


def _dn_constants(reverse, ts):
    i = np.arange(CHUNK)[:, None]
    j = (np.arange(DN_W) % DN_DIM)[None, :]
    tri_incl = (i <= j) if reverse else (i >= j)
    rows = [(i == j).astype(np.float32), np.where(tri_incl, 0.0, -np.inf).astype(np.float32)]
    for lvl in range(1, N_LEVELS + 1):
        half = 1 << (lvl - 1)
        same_blk = (i >> lvl) == (j >> lvl)
        i_hi, j_hi = (i & half) != 0, (j & half) != 0
        rows.append((same_blk & ~i_hi & j_hi if reverse else same_blk & i_hi & ~j_hi).astype(np.float32))
    chunk_masks = np.stack(rows)
    r = np.arange(DN_W)
    mask_bd = (r[:, None] // DN_DIM) == (r[None, :] // DN_DIM)
    bd_f32 = np.stack([mask_bd.astype(np.float32), np.eye(DN_W, dtype=np.float32)])
    goff, boff = (DN_HEADS, 3 * DN_HEADS) if reverse else (0, 2 * DN_HEADS)
    er = np.arange(LANES)[:, None]
    ec = (np.arange(DN_W) // DN_DIM)[None, :]
    expand = np.stack([er == ec + goff, er == ec + boff])
    return (jnp.asarray(chunk_masks), jnp.asarray(chunk_masks[CM_LEVEL0:], BF16), jnp.asarray(bd_f32),
            jnp.asarray(mask_bd, BF16), jnp.asarray(expand, BF16))


def _bd(y, bdb_ref):
    return jnp.concatenate([y] * DN_HEADS, axis=0) * bdb_ref[...]


def _dn_prep(reverse, x_ref, gfeat, cm_ref, cmb_ref, bdf_ref, bdb_ref, exp_ref):
    nchunk = x_ref.shape[1] // CHUNK
    cs = range(nchunk)
    rows = [slice(c * CHUNK, (c + 1) * CHUNK) for c in cs]
    bd = functools.partial(_bd, bdb_ref=bdb_ref)

    kb16 = [x_ref[0, r, DN_W:2 * DN_W] for r in rows]
    gq = [_dot_nt(jnp.concatenate([kb, x_ref[0, r, :DN_W]], axis=0), bd(kb)) for kb, r in zip(kb16, rows)]
    yield
    q = x_ref[0, :, :DN_W].astype(F32)
    k = x_ref[0, :, DN_W:2 * DN_W].astype(F32)
    v = x_ref[0, :, 2 * DN_W:].astype(F32)
    gcol = _dot_exact_lhs(gfeat, exp_ref[0])
    bcol = _dot_exact_lhs(gfeat, exp_ref[1])
    yield
    eye_p = cm_ref[CM_EYE]
    egc = jnp.exp(gcol)
    kbg = k * (bcol * egc)
    vb = v * bcol
    qe = q * egc
    gcs = [gcol[r] for r in rows]
    grow = [jnp.sum(eye_p * g, axis=0, keepdims=True) for g in gcs]
    gtot = [g[0:1, :] if reverse else g[CHUNK - 1:CHUNK, :] for g in gcs]
    decay = [jnp.exp(g - gr + cm_ref[CM_NEG]) for g, gr in zip(gcs, grow)]
    lb = [g[:CHUNK] * bcol[r] * d for g, r, d in zip(gq, rows, decay)]
    ab = [(g[CHUNK:] * d).astype(BF16) for g, d in zip(gq, decay)]
    x = [eye_p - cm_ref[CM_LEVEL0] * l for l in lb]
    lb16 = [l.astype(BF16) for l in lb]
    for lvl in range(1, N_LEVELS):
        xb = [xc.astype(BF16) for xc in x]
        z = [_dot(cmb_ref[lvl] * l, bd(b)) for l, b in zip(lb16, xb)]
        yield
        x = [xc - _dot(b, bd(zc.astype(BF16))) for xc, b, zc in zip(x, xb, z)]
        yield
    xb = [xc.astype(BF16) for xc in x]
    u = [_dot(b, bd(vb[r].astype(BF16))) for b, r in zip(xb, rows)]
    yield
    w = [_dot(b, bd(kbg[r].astype(BF16))) for b, r in zip(xb, rows)]
    yield
    p = [(qe[r] - _dot(a, bd(wc.astype(BF16)))).astype(BF16) for r, a, wc in zip(rows, ab, w)]
    yield
    rr = [_dot(a, bd(uc.astype(BF16))) for a, uc in zip(ab, u)]
    yield
    kd = [(k[r] * jnp.exp(gt - g)).astype(BF16) for r, gt, g in zip(rows, gtot, gcs)]
    full = [_dot_tn(jnp.concatenate([wc, uc], axis=1).astype(BF16), kdc) for wc, uc, kdc in zip(w, u, kd)]
    yield
    bdm = [(bdf_ref[1] * jnp.exp(gt) - bdf_ref[0] * f[:DN_W]).astype(BF16) for gt, f in zip(gtot, full)]
    nfull = [bdf_ref[0] * f[DN_W:] for f in full]
    nt = [nf[0:64] + nf[64:128] + nf[128:192] + nf[192:256] for nf in nfull]
    return p, rr, bdm, nt


def _dn_kernel(reverse, n_tiles, x_ref, g_ref, cm_ref, cmb_ref, bdf_ref, bdb_ref, exp_ref, *rest):
    if reverse:
        z_ref, of_ref, ng_ref, o_ref, st_ref, p_scr, r_scr, m_scr, n_scr = rest
    else:
        o_ref, st_ref, p_scr, r_scr, m_scr, n_scr = rest
    s = pl.program_id(0)
    nchunk = x_ref.shape[1] // CHUNK

    @pl.when(s == 0)
    def _():
        st_ref[...] = jnp.zeros(st_ref.shape, F32)
        p_scr[1] = jnp.zeros(p_scr.shape[1:], BF16)
        r_scr[1] = jnp.zeros(r_scr.shape[1:], F32)
        m_scr[1] = jnp.zeros(m_scr.shape[1:], BF16)
        n_scr[1] = jnp.zeros(n_scr.shape[1:], F32)

    cur = lax.rem(s, 2)
    prev = 1 - cur
    scanned = jnp.maximum(s - 1, 0)
    first = lax.rem(scanned, n_tiles) == 0
    st = jnp.where(first, 0.0, st_ref[...])

    prep = _dn_prep(reverse, x_ref, g_ref[0], cm_ref, cmb_ref, bdf_ref, bdb_ref, exp_ref)

    def scan_step(c, st):
        rows = pl.ds(c * CHUNK, CHUNK)
        sb = st.astype(BF16)
        st_new = _dot(sb, m_scr[prev, c]) + n_scr[prev, rows, :]
        o = _dot_nt(p_scr[prev, rows, :], _bd(sb, bdb_ref)) + r_scr[prev, rows, :]
        return st_new, (rows, o)

    def finish(pending):
        rows, o = pending
        if reverse:
            o = o + of_ref[0, rows, :]
            ms = _head_sumsq(o, bdb_ref[...]) * (1.0 / DN_DIM)
            y = o * lax.rsqrt(ms + RMS_EPS) * ng_ref[...] * _silu(z_ref[0, rows, :].astype(F32))
            o_ref[0, rows, :] = y.astype(BF16)
        else:
            o_ref[0, rows, :] = o

    order = list(reversed(range(nchunk))) if reverse else list(range(nchunk))
    groups = 0
    pending = None
    while True:
        try:
            next(prep)
        except StopIteration as done:
            p, rr, bdm, nt = done.value
            break
        groups += 1
        if groups % 2 == 0 and order:
            if pending is not None:
                finish(pending)
            st, pending = scan_step(order.pop(0), st)
    while order:
        finish(pending)
        st, pending = scan_step(order.pop(0), st)
    finish(pending)
    st_ref[...] = st
    for c in range(nchunk):
        rows = pl.ds(c * CHUNK, CHUNK)
        p_scr[cur, rows, :] = p[c]
        r_scr[cur, rows, :] = rr[c]
        m_scr[cur, c] = bdm[c]
        n_scr[cur, rows, :] = nt[c]


def _delta_pass(reverse, dnqkv, gfeat, z=None, o_fwd=None, norm_g=None):
    bsz, s, _ = dnqkv.shape
    ts = min(DN_TILE, s)
    n = s // ts
    total = bsz * n
    pos = (lambda i: n - 1 - i) if reverse else (lambda i: i)

    def tile_of(flat):
        return flat // n, pos(lax.rem(flat, n)), 0

    prepared = lambda s_: tile_of(jnp.minimum(s_, total - 1))
    scanned = lambda s_: tile_of(jnp.maximum(s_ - 1, 0))
    in_specs = [
        pl.BlockSpec((1, ts, 3 * DN_W), prepared),
        pl.BlockSpec((1, ts, LANES), prepared),
    ]
    consts = _dn_constants(reverse, ts)
    in_specs += [pl.BlockSpec(c.shape, (lambda nd: lambda s_: (0,) * nd)(c.ndim), pipeline_mode=pl.Buffered(1))
                 for c in consts]
    args = [dnqkv, gfeat, *consts]
    if reverse:
        in_specs += [pl.BlockSpec((1, ts, DN_W), scanned), pl.BlockSpec((1, ts, DN_W), scanned),
                     pl.BlockSpec((1, DN_W), lambda s_: (0, 0))]
        args += [z, o_fwd, norm_g]
        out_dtype = BF16
    else:
        out_dtype = F32
    return pl.pallas_call(
        functools.partial(_dn_kernel, reverse, n),
        grid=(total + 1,),
        in_specs=in_specs,
        out_specs=pl.BlockSpec((1, ts, DN_W), scanned),
        out_shape=jax.ShapeDtypeStruct((bsz, s, DN_W), out_dtype),
        scratch_shapes=[pltpu.VMEM((DN_DIM, DN_W), F32),
                        pltpu.VMEM((2, ts, DN_W), BF16),
                        pltpu.VMEM((2, ts, DN_W), F32),
                        pltpu.VMEM((2, ts // CHUNK, DN_W, DN_W), BF16),
                        pltpu.VMEM((2, ts, DN_W), F32)],
        compiler_params=pltpu.CompilerParams(dimension_semantics=("arbitrary",),
                                             vmem_limit_bytes=VMEM_LIMIT_BYTES),
        name="delta_bwd" if reverse else "delta_fwd",
    )(*args)


def _out_mlp_kernel(x_ref, ya_ref, yb_ref, yc_ref, g1_ref, sh2_ref, sc2_ref, g2_ref, wo_a_ref, wo_b_ref,
                    wo_c_ref, ln1g_ref, ln1b_ref, w1_ref, b1_ref, w2_ref, b2_ref, ln2g_ref, ln2b_ref, o_ref):
    tm = x_ref.shape[1]
    groups = [pl.ds(g * (tm // MLP_ROW_GROUPS), tm // MLP_ROW_GROUPS) for g in range(MLP_ROW_GROUPS)]
    y = [_dot(ya_ref[0, r, :], wo_a_ref[...]) + _dot(yb_ref[0, r, :], wo_b_ref[...])
         + _dot(yc_ref[0, r, :], wo_c_ref[...]) for r in groups]
    x1 = [_layer_norm(ALPHA * x_ref[0, r, :] + (1.0 + g1_ref[0]) * yg, ln1g_ref[...], ln1b_ref[...])
          for r, yg in zip(groups, y)]
    h = [(xg * (1.0 + sc2_ref[0]) + sh2_ref[0]).astype(BF16) for xg in x1]
    f = [None] * MLP_ROW_GROUPS
    for c in range(D_FF // FF_CHUNK):
        cols = slice(c * FF_CHUNK, (c + 1) * FF_CHUNK)
        a = [jnp.maximum(_dot(hg, w1_ref[:, cols]) + b1_ref[:, cols], 0.0) for hg in h]
        part = [_dot((ag * ag).astype(BF16), w2_ref[cols, :]) for ag in a]
        f = [pg if fg is None else fg + pg for fg, pg in zip(f, part)]
    for r, xg, fg in zip(groups, x1, f):
        o_ref[0, r, :] = _layer_norm(ALPHA * xg + (1.0 + g2_ref[0]) * (fg + b2_ref[...]),
                                     ln2g_ref[...], ln2b_ref[...])


def _out_mlp(x, ya, yb, yc, mod_l, b0, wo_a, wo_b, wo_c, ln1g, ln1b, w1, b1, w2, b2, ln2g, ln2b):
    bsz, s, _ = x.shape
    tm = min(ROW_TILE, s)
    row = lambda b, i: (b, i, 0)
    full2 = lambda b, i: (0, 0)
    modspec = lambda k: pl.BlockSpec((1, 1, D_MODEL), lambda b, i: (b0 + b, 0, k))
    const = lambda shape: pl.BlockSpec(shape, full2, pipeline_mode=pl.Buffered(1))
    return pl.pallas_call(
        _out_mlp_kernel,
        grid=(bsz, s // tm),
        in_specs=[
            pl.BlockSpec((1, tm, D_MODEL), row),
            pl.BlockSpec((1, tm, CONV_CH), row),
            pl.BlockSpec((1, tm, ATT_Q), row),
            pl.BlockSpec((1, tm, DN_W), row),
            modspec(2), modspec(3), modspec(4), modspec(5),
            const((CONV_CH, D_MODEL)), const((ATT_Q, D_MODEL)), const((DN_W, D_MODEL)),
            const((1, D_MODEL)), const((1, D_MODEL)),
            const((D_MODEL, D_FF)), const((1, D_FF)),
            const((D_FF, D_MODEL)), const((1, D_MODEL)),
            const((1, D_MODEL)), const((1, D_MODEL)),
        ],
        out_specs=pl.BlockSpec((1, tm, D_MODEL), row),
        out_shape=jax.ShapeDtypeStruct((bsz, s, D_MODEL), F32),
        compiler_params=pltpu.CompilerParams(dimension_semantics=("parallel", "parallel"),
                                             vmem_limit_bytes=VMEM_LIMIT_BYTES),
        name="out_mlp",
    )(x, ya, yb, yc, mod_l, mod_l, mod_l, mod_l, wo_a, wo_b, wo_c, ln1g, ln1b, w1, b1, w2, b2, ln2g, ln2b)


def _pad_rows(a, rows):
    return jnp.concatenate([a, jnp.zeros((rows - a.shape[0],) + a.shape[1:], a.dtype)], axis=0)


def _trunk(x, mod, b0, prm):
    for l in range(DEPTH):
        mod_l = mod[l]
        outs = _in_projection(x, mod_l, b0, prm["ln_in_g"], prm["ln_in_b"], prm["w_main"][l], prm["w_gate"][l],
                              prm["conv_a_w"][l], prm["dn_conv_w"][l], prm["gate_params"][l], apply_ln=(l == 0))
        ya, q, kv, dnqkv, z, gfeat = outs[:6]
        if l == 0:
            x = outs[6]
        yb = _attention(q, kv, prm["attn_bias"], prm["sink_log2"][l])
        o_fwd = _delta_pass(False, dnqkv, gfeat)
        yc = _delta_pass(True, dnqkv, gfeat, z=z, o_fwd=o_fwd, norm_g=prm["dn_norm_g"][l])
        x = _out_mlp(x, ya, yb, yc, mod_l, b0, prm["wo_a"][l], prm["wo_b"][l], prm["wo_c"][l],
                     prm["ln1_g"][l], prm["ln1_b"][l], prm["w1"][l], prm["b1"][l], prm["w2"][l], prm["b2"][l],
                     prm["ln2_g"][l], prm["ln2_b"][l])
    return x


def kernel(x_prompt, x_sample, c_prompt, c_sample, ln_in_g, ln_in_b, w_mod, b_mod, w_in, conv_a_w, attn_sink,
           dn_conv_w, dn_a_log_f, dn_a_log_b, dn_dt_bias_f, dn_dt_bias_b, dn_norm_g, w_out, ln1_g, ln1_b,
           w1, b1, w2, b2, ln2_g, ln2_b):
    nb_p, nb_s = c_prompt.shape[0], c_sample.shape[0]
    bp = -(-(nb_p + nb_s) // BF16_SUBLANES) * BF16_SUBLANES
    c_all = _pad_rows(jnp.concatenate([c_prompt, c_sample], axis=0), bp)
    mod = _modulation(c_all, w_mod, b_mod)
    mod = mod.reshape(DEPTH, bp, 1, 6 * D_MODEL)

    row = lambda a: a.reshape(DEPTH, 1, -1)
    zeros4 = jnp.zeros((DEPTH, DN_HEADS), F32)
    gp0 = jnp.concatenate([jnp.exp(dn_a_log_f), jnp.exp(dn_a_log_b), zeros4, zeros4], axis=1)
    gp1 = jnp.concatenate([dn_dt_bias_f, dn_dt_bias_b, zeros4, zeros4], axis=1)
    gate_params = jnp.stack([gp0, gp1], axis=1)
    gate_params = jnp.pad(gate_params, ((0, 0), (0, F32_SUBLANES - 2), (0, LANES - 4 * DN_HEADS)))
    pad_taps = lambda w: jnp.pad(w, ((0, 0), (0, F32_SUBLANES - w.shape[1]), (0, 0)))
    prm = dict(
        ln_in_g=ln_in_g.reshape(1, -1), ln_in_b=ln_in_b.reshape(1, -1),
        w_main=w_in[:, :, :OFF_GATE].astype(BF16),
        w_gate=jnp.pad(w_in[:, :, OFF_GATE:], ((0, 0), (0, 0), (0, LANES - 4 * DN_HEADS))).astype(BF16),
        conv_a_w=pad_taps(conv_a_w), attn_bias=_attention_bias(), sink_log2=attn_sink * LOG2E,
        dn_conv_w=pad_taps(dn_conv_w), gate_params=gate_params,
        dn_norm_g=jnp.tile(dn_norm_g, (1, DN_HEADS)).reshape(DEPTH, 1, DN_W),
        wo_a=w_out[:, :CONV_CH, :].astype(BF16), wo_b=w_out[:, CONV_CH:CONV_CH + ATT_Q, :].astype(BF16),
        wo_c=w_out[:, CONV_CH + ATT_Q:, :].astype(BF16),
        ln1_g=row(ln1_g), ln1_b=row(ln1_b), w1=w1.astype(BF16), b1=row(b1), w2=w2.astype(BF16), b2=row(b2),
        ln2_g=row(ln2_g), ln2_b=row(ln2_b),
    )
    y_prompt = _trunk(x_prompt, mod, 0, prm)
    y_sample = _trunk(x_sample, mod, nb_p, prm)
    return (y_prompt, y_sample)
```

```python
import functools

import numpy as np
import jax
import jax.numpy as jnp
from jax import lax
from jax.experimental import pallas as pl
from jax.experimental.pallas import tpu as pltpu

F32 = jnp.float32
BF16 = jnp.bfloat16

D_MODEL = 1024
DEPTH = 4
CONV_CH = 256
N_HEADS = 8
N_KV_HEADS = 2
HEAD_DIM = 64
WINDOW = 128
DN_HEADS = 4
DN_DIM = 64
DN_W = DN_HEADS * DN_DIM
D_FF = 4 * D_MODEL
ATT_Q = N_HEADS * HEAD_DIM
ATT_KV = N_KV_HEADS * HEAD_DIM
OFF_ATT = 3 * CONV_CH
OFF_DN = OFF_ATT + ATT_Q + 2 * ATT_KV
OFF_GATE = OFF_DN + 4 * DN_W
D_IN = OFF_GATE + 4 * DN_HEADS
ALPHA = (2.0 * DEPTH) ** 0.25
LN_EPS = 1e-5
RMS_EPS = 1e-6
LOG2E = 1.4426950408889634

LANES = 128
BF16_SUBLANES = 16
F32_SUBLANES = 8
VMEM_LIMIT_BYTES = 56 * 1024 * 1024

ROW_TILE = 512
INPROJ_TILE = 1024
INPROJ_POST_GROUPS = 4
ATT_TILE = 1024
ATT_BLOCK = 128
DN_TILE = 512
CHUNK = 64
CUM_ROWS = 256
FF_CHUNK = 1024
MLP_ROW_GROUPS = 2


def _dot(a, b):
    return jnp.dot(a, b, preferred_element_type=F32)


def _dot_nt(a, b):
    return lax.dot_general(a, b, (((1,), (1,)), ((), ())), preferred_element_type=F32)


def _dot_tn(a, b):
    return lax.dot_general(a, b, (((0,), (0,)), ((), ())), preferred_element_type=F32)


def _sigmoid(x):
    return 1.0 / (1.0 + jnp.exp(-x))


def _silu(x):
    return x * _sigmoid(x)


def _softplus(x):
    return jnp.maximum(x, 0.0) + jnp.log(1.0 + jnp.exp(-jnp.abs(x)))


def _layer_norm(x, g, b):
    mu = jnp.mean(x, axis=-1, keepdims=True)
    xc = x - mu
    var = jnp.mean(xc * xc, axis=-1, keepdims=True)
    return xc * lax.rsqrt(var + LN_EPS) * g + b


def _split3(x):
    hi = x.astype(BF16)
    r1 = x - hi.astype(F32)
    mid = r1.astype(BF16)
    lo = (r1 - mid.astype(F32)).astype(BF16)
    return hi, mid, lo


def _dot_exact_rhs(sel, x):
    hi, mid, lo = _split3(x)
    return _dot(sel, hi) + _dot(sel, mid) + _dot(sel, lo)


def _dot_exact_lhs(x, sel):
    hi, mid, lo = _split3(x)
    return _dot(hi, sel) + _dot(mid, sel) + _dot(lo, sel)


def _iota2(shape, axis):
    return lax.broadcasted_iota(jnp.int32, shape, axis)


def _mod_kernel(c_ref, w_ref, b_ref, o_ref):
    s = _silu(c_ref[...]).astype(BF16)
    o_ref[0] = _dot(s, w_ref[0].astype(BF16)) + b_ref[0]


def _modulation(c_all, w_mod, b_mod):
    bp = c_all.shape[0]
    nblk = w_mod.shape[2] // D_MODEL
    return pl.pallas_call(
        _mod_kernel,
        grid=(DEPTH, nblk),
        in_specs=[
            pl.BlockSpec((bp, D_MODEL), lambda l, k: (0, 0)),
            pl.BlockSpec((1, D_MODEL, D_MODEL), lambda l, k: (l, 0, k)),
            pl.BlockSpec((1, 1, D_MODEL), lambda l, k: (l, 0, k)),
        ],
        out_specs=pl.BlockSpec((1, bp, D_MODEL), lambda l, k: (l, 0, k)),
        out_shape=jax.ShapeDtypeStruct((DEPTH, bp, 6 * D_MODEL), F32),
        compiler_params=pltpu.CompilerParams(dimension_semantics=("arbitrary", "arbitrary"),
                                             vmem_limit_bytes=VMEM_LIMIT_BYTES),
        name="modulation",
    )(c_all, w_mod, b_mod.reshape(DEPTH, 1, 6 * D_MODEL))


def _conv3(ext, mid, taps, rows):
    return (taps[1:2, :] * mid + taps[0:1, :] * ext[F32_SUBLANES - 1:F32_SUBLANES - 1 + rows, :]
            + taps[2:3, :] * ext[F32_SUBLANES + 1:F32_SUBLANES + 1 + rows, :])


def _head_sumsq(y, ones_bd):
    y2 = y * y
    hi = y2.astype(BF16)
    lo = (y2 - hi.astype(F32)).astype(BF16)
    return _dot(hi, ones_bd) + _dot(lo, ones_bd)


def _inproj_kernel(apply_ln, x_ref, xp_ref, xn_ref, sh_ref, sc_ref, lng_ref, lnb_ref, w_ref, wg_ref,
                   cwa_ref, cwd_ref, bdb_ref, gp_ref, cum_ref, *rest):
    n_out = 7 if apply_ln else 6
    out_refs, (cu_ext, dn_ext) = rest[:n_out], rest[n_out:]
    o_ya, o_q, o_kv, o_dn, o_z, o_g = out_refs[:6]
    i = pl.program_id(1)
    n = pl.num_programs(1)
    tm = x_ref.shape[1]
    has_prev = (i > 0).astype(F32)
    has_next = (i < n - 1).astype(F32)
    x = x_ref[0]
    xh = jnp.concatenate([xp_ref[0], xn_ref[0]], axis=0)
    if apply_ln:
        x = _layer_norm(x, lng_ref[...], lnb_ref[...])
        xh = _layer_norm(xh, lng_ref[...], lnb_ref[...])
        out_refs[6][0] = x
    h = (x * (1.0 + sc_ref[0]) + sh_ref[0]).astype(BF16)
    hh = (xh * (1.0 + sc_ref[0]) + sh_ref[0]).astype(BF16)
    gates = _dot(h, wg_ref[...])
    gp = gp_ref[...]
    glog = -gp[0:1, :] * _softplus(gates + gp[1:2, :])
    beta = _sigmoid(gates)

    ph_d = _dot(hh, w_ref[:, OFF_DN:OFF_DN + 3 * DN_W])
    ph_a = _dot(hh, w_ref[:, CONV_CH:OFF_ATT])
    pd = _dot(h, w_ref[:, OFF_DN:OFF_DN + 3 * DN_W])

    lane = _iota2((CUM_ROWS, LANES), 1)
    for r0 in range(0, tm, CUM_ROWS):
        g = glog[r0:r0 + CUM_ROWS]
        pre = _dot_exact_rhs(cum_ref[...], g)
        tot = jnp.concatenate([jnp.broadcast_to(pre[c + CHUNK - 1:c + CHUNK], (CHUNK, LANES))
                               for c in range(0, CUM_ROWS, CHUNK)], axis=0)
        o_g[0, r0:r0 + CUM_ROWS, :] = jnp.where(
            lane < DN_HEADS, pre, jnp.where(lane < 2 * DN_HEADS, tot - pre + g, beta[r0:r0 + CUM_ROWS]))

    dn_ext[0:F32_SUBLANES, :] = ph_d[:F32_SUBLANES] * has_prev
    dn_ext[F32_SUBLANES:F32_SUBLANES + tm, :] = pd
    dn_ext[F32_SUBLANES + tm:, :] = ph_d[F32_SUBLANES:] * has_next
    ones_bd = bdb_ref[...]
    grp = tm // INPROJ_POST_GROUPS

    def dn_post(r0):
        mid = dn_ext[F32_SUBLANES + r0:F32_SUBLANES + r0 + grp, :]
        qkv = _silu(_conv3(dn_ext.at[pl.ds(r0, grp + 2 * F32_SUBLANES)], mid, cwd_ref[...], grp))
        q = qkv[:, :DN_W]
        k = qkv[:, DN_W:2 * DN_W]
        rows = pl.ds(r0, grp)
        o_dn[0, rows, :DN_W] = (q * (lax.rsqrt(_head_sumsq(q, ones_bd) + RMS_EPS) * (DN_DIM ** -0.5))).astype(BF16)
        o_dn[0, rows, DN_W:2 * DN_W] = (k * lax.rsqrt(_head_sumsq(k, ones_bd) + RMS_EPS)).astype(BF16)
        o_dn[0, rows, 2 * DN_W:] = qkv[:, 2 * DN_W:].astype(BF16)

    post = [functools.partial(dn_post, g * grp) for g in range(INPROJ_POST_GROUPS)]
    for f in post[:len(post) // 2]:
        f()
    pa = _dot(h, w_ref[:, :OFF_ATT])
    for f in post[len(post) // 2:len(post) * 3 // 4]:
        f()
    o_q[0] = (_dot(h, w_ref[:, OFF_ATT:OFF_ATT + ATT_Q]) * (HEAD_DIM ** -0.5 * LOG2E)).astype(BF16)
    for f in post[len(post) * 3 // 4:]:
        f()
    o_kv[0] = _dot(h, w_ref[:, OFF_ATT + ATT_Q:OFF_DN]).astype(BF16)

    cu = pa[:, CONV_CH:2 * CONV_CH] * pa[:, 2 * CONV_CH:]
    cuh = ph_a[:, :CONV_CH] * ph_a[:, CONV_CH:]
    cu_ext[0:F32_SUBLANES, :] = cuh[:F32_SUBLANES] * has_prev
    cu_ext[F32_SUBLANES:F32_SUBLANES + tm, :] = cu
    cu_ext[F32_SUBLANES + tm:, :] = cuh[F32_SUBLANES:] * has_next
    o_ya[0] = (pa[:, :CONV_CH] * _conv3(cu_ext, cu, cwa_ref[...], tm)).astype(BF16)

    o_z[0] = _dot(h, w_ref[:, OFF_DN + 3 * DN_W:OFF_GATE]).astype(BF16)


def _block_diag_ones():
    r = np.arange(DN_W)
    return jnp.asarray((r[:, None] // DN_DIM) == (r[None, :] // DN_DIM), BF16)


def _chunk_cumsum_selector():
    t = np.arange(CUM_ROWS)
    same_chunk = (t[:, None] // CHUNK) == (t[None, :] // CHUNK)
    return jnp.asarray(same_chunk & (t[None, :] <= t[:, None]), BF16)


def _in_projection(x, mod_l, b0, ln_g, ln_b, w_main, w_gate, conv_a_w, dn_conv_w, gate_params, apply_ln):
    bsz, s, _ = x.shape
    tm = min(INPROJ_TILE, s)
    assert s % tm == 0 and tm % CUM_ROWS == 0, (s, tm)
    nh = tm // F32_SUBLANES
    row = lambda b, i: (b, i, 0)
    full2 = lambda b, i: (0, 0)
    const = lambda shape: pl.BlockSpec(shape, full2, pipeline_mode=pl.Buffered(1))
    widths = (CONV_CH, ATT_Q, 2 * ATT_KV, 3 * DN_W, DN_W, LANES)
    dtypes = (BF16, BF16, BF16, BF16, BF16, F32)
    out_shape = [jax.ShapeDtypeStruct((bsz, s, w), dt) for w, dt in zip(widths, dtypes)]
    out_specs = [pl.BlockSpec((1, tm, w), row) for w in widths]
    if apply_ln:
        out_shape.append(jax.ShapeDtypeStruct((bsz, s, D_MODEL), F32))
        out_specs.append(pl.BlockSpec((1, tm, D_MODEL), row))
    return pl.pallas_call(
        functools.partial(_inproj_kernel, apply_ln),
        grid=(bsz, s // tm),
        in_specs=[
            pl.BlockSpec((1, tm, D_MODEL), row),
            pl.BlockSpec((1, F32_SUBLANES, D_MODEL), lambda b, i: (b, jnp.maximum(i * nh - 1, 0), 0)),
            pl.BlockSpec((1, F32_SUBLANES, D_MODEL),
                         lambda b, i: (b, jnp.minimum((i + 1) * nh, s // F32_SUBLANES - 1), 0)),
            pl.BlockSpec((1, 1, D_MODEL), lambda b, i: (b0 + b, 0, 0)),
            pl.BlockSpec((1, 1, D_MODEL), lambda b, i: (b0 + b, 0, 1)),
            pl.BlockSpec((1, D_MODEL), full2),
            pl.BlockSpec((1, D_MODEL), full2),
            const((D_MODEL, OFF_GATE)),
            const((D_MODEL, LANES)),
            const((F32_SUBLANES, CONV_CH)),
            const((F32_SUBLANES, 3 * DN_W)),
            const((DN_W, DN_W)),
            const((F32_SUBLANES, LANES)),
            const((CUM_ROWS, CUM_ROWS)),
        ],
        out_specs=out_specs,
        out_shape=out_shape,
        scratch_shapes=[pltpu.VMEM((tm + 2 * F32_SUBLANES, CONV_CH), F32),
                        pltpu.VMEM((tm + 2 * F32_SUBLANES, 3 * DN_W), F32)],
        compiler_params=pltpu.CompilerParams(dimension_semantics=("parallel", "parallel"),
                                             vmem_limit_bytes=VMEM_LIMIT_BYTES),
        name="in_projection",
    )(x, x, x, mod_l, mod_l, ln_g, ln_b, w_main, w_gate, conv_a_w, dn_conv_w, _block_diag_ones(), gate_params,
      _chunk_cumsum_selector())


def _attention_bias():
    qi = jnp.arange(ATT_BLOCK)[:, None]
    ki = jnp.arange(3 * ATT_BLOCK)[None, :]
    dist = jnp.abs(ki - ATT_BLOCK - qi)
    slopes = jnp.exp2(-8.0 * jnp.arange(1, N_HEADS + 1, dtype=F32) / N_HEADS)
    base = -(slopes * LOG2E)[:, None, None] * dist.astype(F32)[None]
    variants = []
    for var in range(4):
        valid = dist <= WINDOW
        if var & 1:
            valid = valid & (ki >= ATT_BLOCK)
        if var & 2:
            valid = valid & (ki < 2 * ATT_BLOCK)
        variants.append(jnp.where(valid[None], base, -jnp.inf))
    return jnp.stack(variants)


def _attention_kernel(q_ref, kv_ref, kv_prev_ref, kv_next_ref, bias_ref, sink_ref, o_ref,
                      ka_ref, kb_ref, va_ref, vb_ref):
    i = pl.program_id(1)
    n = pl.num_programs(1)
    tq = q_ref.shape[1]

    def fill(r0, blk):
        kk = blk[:, :LANES].astype(F32)
        vv = blk[:, LANES:].astype(F32)
        lo = _iota2(kk.shape, 1) < HEAD_DIM
        kks = pltpu.roll(kk, HEAD_DIM, 1)
        vvs = pltpu.roll(vv, HEAD_DIM, 1)
        rows = pl.ds(r0, blk.shape[0])
        ka_ref[0, rows, :] = jnp.where(lo, kk, 0.0).astype(BF16)
        kb_ref[0, rows, :] = jnp.where(lo, 0.0, kks).astype(BF16)
        ka_ref[1, rows, :] = jnp.where(lo, kks, 0.0).astype(BF16)
        kb_ref[1, rows, :] = jnp.where(lo, 0.0, kk).astype(BF16)
        va_ref[0, rows, :] = jnp.where(lo, vv, 0.0).astype(BF16)
        vb_ref[0, rows, :] = jnp.where(lo, 0.0, vvs).astype(BF16)
        va_ref[1, rows, :] = jnp.where(lo, vvs, 0.0).astype(BF16)
        vb_ref[1, rows, :] = jnp.where(lo, 0.0, vv).astype(BF16)

    fill(0, kv_prev_ref[0])
    fill(ATT_BLOCK, kv_ref[0])
    fill(ATT_BLOCK + tq, kv_next_ref[0])

    nwin = 3 * ATT_BLOCK
    nsub = tq // ATT_BLOCK
    lane_lo = _iota2((ATT_BLOCK, LANES), 1) < HEAD_DIM
    grp = N_HEADS // N_KV_HEADS

    units = [(j, hk) for j in range(nsub) for hk in range(N_KV_HEADS)]

    def scores(j, hk):
        blk = i * nsub + j
        var = jnp.where(blk == 0, 1, 0) + jnp.where(blk == n * nsub - 1, 2, 0)
        win = pl.ds(j * ATT_BLOCK, nwin)
        qrows = pl.ds(j * ATT_BLOCK, ATT_BLOCK)
        out = []
        for h in range(hk * grp, (hk + 1) * grp):
            qp = q_ref[0, qrows, (h // 2) * LANES:(h // 2 + 1) * LANES]
            kz = (kb_ref if h % 2 else ka_ref)[hk, win, :]
            out.append(_dot_nt(qp, kz) + bias_ref[var, h])
        return out

    sc = scores(*units[0])
    for u, (j, hk) in enumerate(units):
        sc_next = scores(*units[u + 1]) if u + 1 < len(units) else None
        win = pl.ds(j * ATT_BLOCK, nwin)
        qrows = pl.ds(j * ATT_BLOCK, ATT_BLOCK)
        hs = range(hk * grp, (hk + 1) * grp)
        m = [jnp.maximum(jnp.max(s_, axis=-1, keepdims=True), sink_ref[h]) for s_, h in zip(sc, hs)]
        e = [jnp.exp2(s_ - m_) for s_, m_ in zip(sc, m)]
        rcp = [1.0 / (jnp.sum(e_, axis=-1, keepdims=True) + jnp.exp2(sink_ref[h] - m_))
               for e_, m_, h in zip(e, m, hs)]
        pv = [_dot(e_.astype(BF16), (vb_ref if h % 2 else va_ref)[hk, win, :]) for e_, h in zip(e, hs)]
        for t in range(grp // 2):
            pair = hk * (grp // 2) + t
            out = (pv[2 * t] + pv[2 * t + 1]) * jnp.where(lane_lo, rcp[2 * t], rcp[2 * t + 1])
            o_ref[0, qrows, pair * LANES:(pair + 1) * LANES] = out.astype(BF16)
        sc = sc_next


def _attention(q, kv, bias, sink_log2):
    bsz, s, _ = q.shape
    tq = min(ATT_TILE, s)
    nb = tq // ATT_BLOCK
    row = lambda b, i: (b, i, 0)
    return pl.pallas_call(
        _attention_kernel,
        grid=(bsz, s // tq),
        in_specs=[
            pl.BlockSpec((1, tq, ATT_Q), row),
            pl.BlockSpec((1, tq, 2 * ATT_KV), row),
            pl.BlockSpec((1, ATT_BLOCK, 2 * ATT_KV), lambda b, i: (b, jnp.maximum(i * nb - 1, 0), 0)),
            pl.BlockSpec((1, ATT_BLOCK, 2 * ATT_KV),
                         lambda b, i: (b, jnp.minimum((i + 1) * nb, s // ATT_BLOCK - 1), 0)),
            pl.BlockSpec(bias.shape, lambda b, i: (0, 0, 0, 0), pipeline_mode=pl.Buffered(1)),
            pl.BlockSpec(memory_space=pltpu.SMEM),
        ],
        out_specs=pl.BlockSpec((1, tq, ATT_Q), row),
        out_shape=jax.ShapeDtypeStruct((bsz, s, ATT_Q), BF16),
        scratch_shapes=[
            pltpu.VMEM((N_KV_HEADS, tq + 2 * ATT_BLOCK, LANES), BF16),
            pltpu.VMEM((N_KV_HEADS, tq + 2 * ATT_BLOCK, LANES), BF16),
            pltpu.VMEM((N_KV_HEADS, tq + 2 * ATT_BLOCK, LANES), BF16),
            pltpu.VMEM((N_KV_HEADS, tq + 2 * ATT_BLOCK, LANES), BF16),
        ],
        compiler_params=pltpu.CompilerParams(dimension_semantics=("parallel", "parallel"),
                                             vmem_limit_bytes=VMEM_LIMIT_BYTES),
        name="window_attention",
    )(q, kv, kv, kv, bias, sink_log2)


N_LEVELS = 6
CM_EYE, CM_NEG, CM_LEVEL0 = 0, 1, 2


def _dn_constants(reverse, ts):
    i = np.arange(CHUNK)[:, None]
    j = (np.arange(DN_W) % DN_DIM)[None, :]
    tri_incl = (i <= j) if reverse else (i >= j)
    rows = [(i == j).astype(np.float32), np.where(tri_incl, 0.0, -np.inf).astype(np.float32)]
    for lvl in range(1, N_LEVELS + 1):
        half = 1 << (lvl - 1)
        same_blk = (i >> lvl) == (j >> lvl)
        i_hi, j_hi = (i & half) != 0, (j & half) != 0
        rows.append((same_blk & ~i_hi & j_hi if reverse else same_blk & i_hi & ~j_hi).astype(np.float32))
    chunk_masks = np.stack(rows)
    r = np.arange(DN_W)
    mask_bd = (r[:, None] // DN_DIM) == (r[None, :] // DN_DIM)
    bd_f32 = np.stack([mask_bd.astype(np.float32), np.eye(DN_W, dtype=np.float32)])
    goff, boff = (DN_HEADS, 3 * DN_HEADS) if reverse else (0, 2 * DN_HEADS)
    er = np.arange(LANES)[:, None]
    ec = (np.arange(DN_W) // DN_DIM)[None, :]
    expand = np.stack([er == ec + goff, er == ec + boff])
    return (jnp.asarray(chunk_masks), jnp.asarray(chunk_masks[CM_LEVEL0:], BF16), jnp.asarray(bd_f32),
            jnp.asarray(mask_bd, BF16), jnp.asarray(expand, BF16))


def _bd(y, bdb_ref):
    return jnp.concatenate([y] * DN_HEADS, axis=0) * bdb_ref[...]


def _dn_prep(reverse, x_ref, gfeat, cm_ref, cmb_ref, bdf_ref, bdb_ref, exp_ref):
    nchunk = x_ref.shape[1] // CHUNK
    cs = range(nchunk)
    rows = [slice(c * CHUNK, (c + 1) * CHUNK) for c in cs]
    bd = functools.partial(_bd, bdb_ref=bdb_ref)

    kb16 = [x_ref[0, r, DN_W:2 * DN_W] for r in rows]
    gq = [_dot_nt(jnp.concatenate([kb, x_ref[0, r, :DN_W]], axis=0), bd(kb)) for kb, r in zip(kb16, rows)]
    yield
    q = x_ref[0, :, :DN_W].astype(F32)
    k = x_ref[0, :, DN_W:2 * DN_W].astype(F32)
    v = x_ref[0, :, 2 * DN_W:].astype(F32)
    g_hi, g_mid, g_lo = _split3(gfeat)
    gcol = _dot(g_hi, exp_ref[0]) + _dot(g_mid, exp_ref[0]) + _dot(g_lo, exp_ref[0])
    bcol = _dot(g_hi, exp_ref[1]) + _dot(g_mid, exp_ref[1]) + _dot(g_lo, exp_ref[1])
    yield
    eye_p = cm_ref[CM_EYE]
    egc = jnp.exp(gcol)
    kbg = k * (bcol * egc)
    vb = v * bcol
    qe = q * egc
    gcs = [gcol[r] for r in rows]
    grow = [jnp.sum(eye_p * g, axis=0, keepdims=True) for g in gcs]
    gtot = [g[0:1, :] if reverse else g[CHUNK - 1:CHUNK, :] for g in gcs]
    decay = [jnp.exp(g - gr + cm_ref[CM_NEG]) for g, gr in zip(gcs, grow)]
    lb = [g[:CHUNK] * bcol[r] * d for g, r, d in zip(gq, rows, decay)]
    ab = [(g[CHUNK:] * d).astype(BF16) for g, d in zip(gq, decay)]
    x = [eye_p - cm_ref[CM_LEVEL0] * l for l in lb]
    lb16 = [l.astype(BF16) for l in lb]
    for lvl in range(1, N_LEVELS):
        xb = [xc.astype(BF16) for xc in x]
        z = [_dot(cmb_ref[lvl] * l, bd(b)) for l, b in zip(lb16, xb)]
        yield
        x = [xc - _dot(b, bd(zc.astype(BF16))) for xc, b, zc in zip(x, xb, z)]
        yield
    xb = [xc.astype(BF16) for xc in x]
    u = [_dot(b, bd(vb[r].astype(BF16))) for b, r in zip(xb, rows)]
    yield
    w = [_dot(b, bd(kbg[r].astype(BF16))) for b, r in zip(xb, rows)]
    yield
    p = [(qe[r] - _dot(a, bd(wc.astype(BF16)))).astype(BF16) for r, a, wc in zip(rows, ab, w)]
    yield
    rr = [_dot(a, bd(uc.astype(BF16))) for a, uc in zip(ab, u)]
    yield
    kd = [(k[r] * jnp.exp(gt - g)).astype(BF16) for r, gt, g in zip(rows, gtot, gcs)]
    full = [_dot_tn(jnp.concatenate([wc, uc], axis=1).astype(BF16), kdc) for wc, uc, kdc in zip(w, u, kd)]
    yield
    bdm = [(bdf_ref[1] * jnp.exp(gt) - bdf_ref[0] * f[:DN_W]).astype(BF16) for gt, f in zip(gtot, full)]
    nfull = [bdf_ref[0] * f[DN_W:] for f in full]
    nt = [nf[0:64] + nf[64:128] + nf[128:192] + nf[192:256] for nf in nfull]
    return p, rr, bdm, nt


def _dn_kernel(reverse, n_tiles, x_ref, g_ref, cm_ref, cmb_ref, bdf_ref, bdb_ref, exp_ref, *rest):
    if reverse:
        z_ref, of_ref, ng_ref, o_ref, st_ref, p_scr, r_scr, m_scr, n_scr = rest
    else:
        o_ref, st_ref, p_scr, r_scr, m_scr, n_scr = rest
    s = pl.program_id(0)
    nchunk = x_ref.shape[1] // CHUNK

    @pl.when(s == 0)
    def _():
        st_ref[...] = jnp.zeros(st_ref.shape, F32)
        p_scr[1] = jnp.zeros(p_scr.shape[1:], BF16)
        r_scr[1] = jnp.zeros(r_scr.shape[1:], F32)
        m_scr[1] = jnp.zeros(m_scr.shape[1:], BF16)
        n_scr[1] = jnp.zeros(n_scr.shape[1:], F32)

    cur = lax.rem(s, 2)
    prev = 1 - cur
    scanned = jnp.maximum(s - 1, 0)
    first = lax.rem(scanned, n_tiles) == 0
    st = jnp.where(first, 0.0, st_ref[...])

    prep = _dn_prep(reverse, x_ref, g_ref[0], cm_ref, cmb_ref, bdf_ref, bdb_ref, exp_ref)

    def scan_step(c, st):
        rows = pl.ds(c * CHUNK, CHUNK)
        sb = st.astype(BF16)
        st_new = _dot(sb, m_scr[prev, c]) + n_scr[prev, rows, :]
        o = _dot_nt(p_scr[prev, rows, :], _bd(sb, bdb_ref)) + r_scr[prev, rows, :]
        return st_new, (rows, o)

    def finish(pending):
        rows, o = pending
        if reverse:
            o = o + of_ref[0, rows, :]
            ms = _head_sumsq(o, bdb_ref[...]) * (1.0 / DN_DIM)
            y = o * lax.rsqrt(ms + RMS_EPS) * ng_ref[...] * _silu(z_ref[0, rows, :].astype(F32))
            o_ref[0, rows, :] = y.astype(BF16)
        else:
            o_ref[0, rows, :] = o

    order = list(reversed(range(nchunk))) if reverse else list(range(nchunk))
    groups = 0
    pending = None
    while True:
        try:
            next(prep)
        except StopIteration as done:
            p, rr, bdm, nt = done.value
            break
        groups += 1
        if groups % 2 == 0 and order:
            if pending is not None:
                finish(pending)
            st, pending = scan_step(order.pop(0), st)
    while order:
        finish(pending)
        st, pending = scan_step(order.pop(0), st)
    finish(pending)
    st_ref[...] = st
    for c in range(nchunk):
        rows = pl.ds(c * CHUNK, CHUNK)
        p_scr[cur, rows, :] = p[c]
        r_scr[cur, rows, :] = rr[c]
        m_scr[cur, c] = bdm[c]
        n_scr[cur, rows, :] = nt[c]


def _delta_pass(reverse, dnqkv, gfeat, z=None, o_fwd=None, norm_g=None):
    bsz, s, _ = dnqkv.shape
    ts = min(DN_TILE, s)
    n = s // ts
    total = bsz * n
    pos = (lambda i: n - 1 - i) if reverse else (lambda i: i)

    def tile_of(flat):
        return flat // n, pos(lax.rem(flat, n)), 0

    prepared = lambda s_: tile_of(jnp.minimum(s_, total - 1))
    scanned = lambda s_: tile_of(jnp.maximum(s_ - 1, 0))
    in_specs = [
        pl.BlockSpec((1, ts, 3 * DN_W), prepared),
        pl.BlockSpec((1, ts, LANES), prepared),
    ]
    consts = _dn_constants(reverse, ts)
    in_specs += [pl.BlockSpec(c.shape, (lambda nd: lambda s_: (0,) * nd)(c.ndim), pipeline_mode=pl.Buffered(1))
                 for c in consts]
    args = [dnqkv, gfeat, *consts]
    if reverse:
        in_specs += [pl.BlockSpec((1, ts, DN_W), scanned), pl.BlockSpec((1, ts, DN_W), scanned),
                     pl.BlockSpec((1, DN_W), lambda s_: (0, 0))]
        args += [z, o_fwd, norm_g]
        out_dtype = BF16
    else:
        out_dtype = F32
    return pl.pallas_call(
        functools.partial(_dn_kernel, reverse, n),
        grid=(total + 1,),
        in_specs=in_specs,
        out_specs=pl.BlockSpec((1, ts, DN_W), scanned),
        out_shape=jax.ShapeDtypeStruct((bsz, s, DN_W), out_dtype),
        scratch_shapes=[pltpu.VMEM((DN_DIM, DN_W), F32),
                        pltpu.VMEM((2, ts, DN_W), BF16),
                        pltpu.VMEM((2, ts, DN_W), F32),
                        pltpu.VMEM((2, ts // CHUNK, DN_W, DN_W), BF16),
                        pltpu.VMEM((2, ts, DN_W), F32)],
        compiler_params=pltpu.CompilerParams(dimension_semantics=("arbitrary",),
                                             vmem_limit_bytes=VMEM_LIMIT_BYTES),
        name="delta_bwd" if reverse else "delta_fwd",
    )(*args)


def _out_mlp_kernel(x_ref, ya_ref, yb_ref, yc_ref, g1_ref, sh2_ref, sc2_ref, g2_ref, wo_a_ref, wo_b_ref,
                    wo_c_ref, ln1g_ref, ln1b_ref, w1_ref, b1_ref, w2_ref, b2_ref, ln2g_ref, ln2b_ref, o_ref):
    tm = x_ref.shape[1]
    groups = [pl.ds(g * (tm // MLP_ROW_GROUPS), tm // MLP_ROW_GROUPS) for g in range(MLP_ROW_GROUPS)]
    y = [_dot(ya_ref[0, r, :], wo_a_ref[...]) + _dot(yb_ref[0, r, :], wo_b_ref[...])
         + _dot(yc_ref[0, r, :], wo_c_ref[...]) for r in groups]
    x1 = [_layer_norm(ALPHA * x_ref[0, r, :] + (1.0 + g1_ref[0]) * yg, ln1g_ref[...], ln1b_ref[...])
          for r, yg in zip(groups, y)]
    h = [(xg * (1.0 + sc2_ref[0]) + sh2_ref[0]).astype(BF16) for xg in x1]
    f = [None] * MLP_ROW_GROUPS
    for c in range(D_FF // FF_CHUNK):
        cols = slice(c * FF_CHUNK, (c + 1) * FF_CHUNK)
        a = [jnp.maximum(_dot(hg, w1_ref[:, cols]) + b1_ref[:, cols], 0.0) for hg in h]
        part = [_dot((ag * ag).astype(BF16), w2_ref[cols, :]) for ag in a]
        f = [pg if fg is None else fg + pg for fg, pg in zip(f, part)]
    for r, xg, fg in zip(groups, x1, f):
        o_ref[0, r, :] = _layer_norm(ALPHA * xg + (1.0 + g2_ref[0]) * (fg + b2_ref[...]),
                                     ln2g_ref[...], ln2b_ref[...])


def _out_mlp(x, ya, yb, yc, mod_l, b0, wo_a, wo_b, wo_c, ln1g, ln1b, w1, b1, w2, b2, ln2g, ln2b):
    bsz, s, _ = x.shape
    tm = min(ROW_TILE, s)
    row = lambda b, i: (b, i, 0)
    full2 = lambda b, i: (0, 0)
    modspec = lambda k: pl.BlockSpec((1, 1, D_MODEL), lambda b, i: (b0 + b, 0, k))
    const = lambda shape: pl.BlockSpec(shape, full2, pipeline_mode=pl.Buffered(1))
    return pl.pallas_call(
        _out_mlp_kernel,
        grid=(bsz, s // tm),
        in_specs=[
            pl.BlockSpec((1, tm, D_MODEL), row),
            pl.BlockSpec((1, tm, CONV_CH), row),
            pl.BlockSpec((1, tm, ATT_Q), row),
            pl.BlockSpec((1, tm, DN_W), row),
            modspec(2), modspec(3), modspec(4), modspec(5),
            const((CONV_CH, D_MODEL)), const((ATT_Q, D_MODEL)), const((DN_W, D_MODEL)),
            const((1, D_MODEL)), const((1, D_MODEL)),
            const((D_MODEL, D_FF)), const((1, D_FF)),
            const((D_FF, D_MODEL)), const((1, D_MODEL)),
            const((1, D_MODEL)), const((1, D_MODEL)),
        ],
        out_specs=pl.BlockSpec((1, tm, D_MODEL), row),
        out_shape=jax.ShapeDtypeStruct((bsz, s, D_MODEL), F32),
        compiler_params=pltpu.CompilerParams(dimension_semantics=("parallel", "parallel"),
                                             vmem_limit_bytes=VMEM_LIMIT_BYTES),
        name="out_mlp",
    )(x, ya, yb, yc, mod_l, mod_l, mod_l, mod_l, wo_a, wo_b, wo_c, ln1g, ln1b, w1, b1, w2, b2, ln2g, ln2b)


def _pad_rows(a, rows):
    return jnp.concatenate([a, jnp.zeros((rows - a.shape[0],) + a.shape[1:], a.dtype)], axis=0)


def _trunk(x, mod, b0, prm):
    for l in range(DEPTH):
        mod_l = mod[l]
        outs = _in_projection(x, mod_l, b0, prm["ln_in_g"], prm["ln_in_b"], prm["w_main"][l], prm["w_gate"][l],
                              prm["conv_a_w"][l], prm["dn_conv_w"][l], prm["gate_params"][l], apply_ln=(l == 0))
        ya, q, kv, dnqkv, z, gfeat = outs[:6]
        if l == 0:
            x = outs[6]
        yb = _attention(q, kv, prm["attn_bias"], prm["sink_log2"][l])
        o_fwd = _delta_pass(False, dnqkv, gfeat)
        yc = _delta_pass(True, dnqkv, gfeat, z=z, o_fwd=o_fwd, norm_g=prm["dn_norm_g"][l])
        x = _out_mlp(x, ya, yb, yc, mod_l, b0, prm["wo_a"][l], prm["wo_b"][l], prm["wo_c"][l],
                     prm["ln1_g"][l], prm["ln1_b"][l], prm["w1"][l], prm["b1"][l], prm["w2"][l], prm["b2"][l],
                     prm["ln2_g"][l], prm["ln2_b"][l])
    return x


def kernel(x_prompt, x_sample, c_prompt, c_sample, ln_in_g, ln_in_b, w_mod, b_mod, w_in, conv_a_w, attn_sink,
           dn_conv_w, dn_a_log_f, dn_a_log_b, dn_dt_bias_f, dn_dt_bias_b, dn_norm_g, w_out, ln1_g, ln1_b,
           w1, b1, w2, b2, ln2_g, ln2_b):
    nb_p, nb_s = c_prompt.shape[0], c_sample.shape[0]
    bp = -(-(nb_p + nb_s) // BF16_SUBLANES) * BF16_SUBLANES
    c_all = _pad_rows(jnp.concatenate([c_prompt, c_sample], axis=0), bp)
    mod = _modulation(c_all, w_mod, b_mod)
    mod = mod.reshape(DEPTH, bp, 1, 6 * D_MODEL)

    row = lambda a: a.reshape(DEPTH, 1, -1)
    zeros4 = jnp.zeros((DEPTH, DN_HEADS), F32)
    gp0 = jnp.concatenate([jnp.exp(dn_a_log_f), jnp.exp(dn_a_log_b), zeros4, zeros4], axis=1)
    gp1 = jnp.concatenate([dn_dt_bias_f, dn_dt_bias_b, zeros4, zeros4], axis=1)
    gate_params = jnp.stack([gp0, gp1], axis=1)
    gate_params = jnp.pad(gate_params, ((0, 0), (0, F32_SUBLANES - 2), (0, LANES - 4 * DN_HEADS)))
    pad_taps = lambda w: jnp.pad(w, ((0, 0), (0, F32_SUBLANES - w.shape[1]), (0, 0)))
    prm = dict(
        ln_in_g=ln_in_g.reshape(1, -1), ln_in_b=ln_in_b.reshape(1, -1),
        w_main=w_in[:, :, :OFF_GATE].astype(BF16),
        w_gate=jnp.pad(w_in[:, :, OFF_GATE:], ((0, 0), (0, 0), (0, LANES - 4 * DN_HEADS))).astype(BF16),
        conv_a_w=pad_taps(conv_a_w), attn_bias=_attention_bias(), sink_log2=attn_sink * LOG2E,
        dn_conv_w=pad_taps(dn_conv_w), gate_params=gate_params,
        dn_norm_g=jnp.tile(dn_norm_g, (1, DN_HEADS)).reshape(DEPTH, 1, DN_W),
        wo_a=w_out[:, :CONV_CH, :].astype(BF16), wo_b=w_out[:, CONV_CH:CONV_CH + ATT_Q, :].astype(BF16),
        wo_c=w_out[:, CONV_CH + ATT_Q:, :].astype(BF16),
        ln1_g=row(ln1_g), ln1_b=row(ln1_b), w1=w1.astype(BF16), b1=row(b1), w2=w2.astype(BF16), b2=row(b2),
        ln2_g=row(ln2_g), ln2_b=row(ln2_b),
    )
    y_prompt = _trunk(x_prompt, mod, 0, prm)
    y_sample = _trunk(x_sample, mod, nb_p, prm)
    return (y_prompt, y_sample)
```

```python
import functools

import numpy as np
import jax
import jax.numpy as jnp
from jax import lax
from jax.experimental import pallas as pl
from jax.experimental.pallas import tpu as pltpu

F32 = jnp.float32
BF16 = jnp.bfloat16

D_MODEL = 1024
DEPTH = 4
CONV_CH = 256
N_HEADS = 8
N_KV_HEADS = 2
HEAD_DIM = 64
WINDOW = 128
DN_HEADS = 4
DN_DIM = 64
DN_W = DN_HEADS * DN_DIM
D_FF = 4 * D_MODEL
ATT_Q = N_HEADS * HEAD_DIM
ATT_KV = N_KV_HEADS * HEAD_DIM
OFF_ATT = 3 * CONV_CH
OFF_DN = OFF_ATT + ATT_Q + 2 * ATT_KV
OFF_GATE = OFF_DN + 4 * DN_W
D_IN = OFF_GATE + 4 * DN_HEADS
ALPHA = (2.0 * DEPTH) ** 0.25
LN_EPS = 1e-5
RMS_EPS = 1e-6
LOG2E = 1.4426950408889634

LANES = 128
BF16_SUBLANES = 16
F32_SUBLANES = 8
VMEM_LIMIT_BYTES = 56 * 1024 * 1024

ROW_TILE = 512
INPROJ_TILE = 1024
INPROJ_POST_GROUPS = 4
ATT_TILE = 2048
ATT_BLOCK = 128
DN_TILE = 512
CHUNK = 64
CUM_ROWS = 256
FF_CHUNK = 1024
MLP_ROW_GROUPS = 2


def _dot(a, b):
    return jnp.dot(a, b, preferred_element_type=F32)


def _dot_nt(a, b):
    return lax.dot_general(a, b, (((1,), (1,)), ((), ())), preferred_element_type=F32)


def _dot_tn(a, b):
    return lax.dot_general(a, b, (((0,), (0,)), ((), ())), preferred_element_type=F32)


def _sigmoid(x):
    return 1.0 / (1.0 + jnp.exp(-x))


def _silu(x):
    return x * _sigmoid(x)


def _softplus(x):
    return jnp.maximum(x, 0.0) + jnp.log(1.0 + jnp.exp(-jnp.abs(x)))


def _layer_norm(x, g, b):
    mu = jnp.mean(x, axis=-1, keepdims=True)
    xc = x - mu
    var = jnp.mean(xc * xc, axis=-1, keepdims=True)
    return xc * lax.rsqrt(var + LN_EPS) * g + b


def _split3(x):
    hi = x.astype(BF16)
    r1 = x - hi.astype(F32)
    mid = r1.astype(BF16)
    lo = (r1 - mid.astype(F32)).astype(BF16)
    return hi, mid, lo


def _dot_exact_rhs(sel, x):
    hi, mid, lo = _split3(x)
    return _dot(sel, hi) + _dot(sel, mid) + _dot(sel, lo)


def _dot_exact_lhs(x, sel):
    hi, mid, lo = _split3(x)
    return _dot(hi, sel) + _dot(mid, sel) + _dot(lo, sel)


def _iota2(shape, axis):
    return lax.broadcasted_iota(jnp.int32, shape, axis)


def _mod_kernel(c_ref, w_ref, b_ref, o_ref):
    s = _silu(c_ref[...]).astype(BF16)
    o_ref[0] = _dot(s, w_ref[0].astype(BF16)) + b_ref[0]


def _modulation(c_all, w_mod, b_mod):
    bp = c_all.shape[0]
    nblk = w_mod.shape[2] // D_MODEL
    return pl.pallas_call(
        _mod_kernel,
        grid=(DEPTH, nblk),
        in_specs=[
            pl.BlockSpec((bp, D_MODEL), lambda l, k: (0, 0)),
            pl.BlockSpec((1, D_MODEL, D_MODEL), lambda l, k: (l, 0, k)),
            pl.BlockSpec((1, 1, D_MODEL), lambda l, k: (l, 0, k)),
        ],
        out_specs=pl.BlockSpec((1, bp, D_MODEL), lambda l, k: (l, 0, k)),
        out_shape=jax.ShapeDtypeStruct((DEPTH, bp, 6 * D_MODEL), F32),
        compiler_params=pltpu.CompilerParams(dimension_semantics=("arbitrary", "arbitrary"),
                                             vmem_limit_bytes=VMEM_LIMIT_BYTES),
        name="modulation",
    )(c_all, w_mod, b_mod.reshape(DEPTH, 1, 6 * D_MODEL))


def _conv3(ext, mid, taps, rows):
    return (taps[1:2, :] * mid + taps[0:1, :] * ext[F32_SUBLANES - 1:F32_SUBLANES - 1 + rows, :]
            + taps[2:3, :] * ext[F32_SUBLANES + 1:F32_SUBLANES + 1 + rows, :])


def _head_sumsq(y, ones_bd):
    y2 = y * y
    hi = y2.astype(BF16)
    lo = (y2 - hi.astype(F32)).astype(BF16)
    return _dot(hi, ones_bd) + _dot(lo, ones_bd)


def _inproj_kernel(apply_ln, x_ref, xp_ref, xn_ref, sh_ref, sc_ref, lng_ref, lnb_ref, w_ref, wg_ref,
                   cwa_ref, cwd_ref, bdb_ref, gp_ref, cum_ref, *rest):
    n_out = 7 if apply_ln else 6
    out_refs, (cu_ext, dn_ext) = rest[:n_out], rest[n_out:]
    o_ya, o_q, o_kv, o_dn, o_z, o_g = out_refs[:6]
    i = pl.program_id(1)
    n = pl.num_programs(1)
    tm = x_ref.shape[1]
    has_prev = (i > 0).astype(F32)
    has_next = (i < n - 1).astype(F32)
    x = x_ref[0]
    xh = jnp.concatenate([xp_ref[0], xn_ref[0]], axis=0)
    if apply_ln:
        x = _layer_norm(x, lng_ref[...], lnb_ref[...])
        xh = _layer_norm(xh, lng_ref[...], lnb_ref[...])
        out_refs[6][0] = x
    h = (x * (1.0 + sc_ref[0]) + sh_ref[0]).astype(BF16)
    hh = (xh * (1.0 + sc_ref[0]) + sh_ref[0]).astype(BF16)
    gates = _dot(h, wg_ref[...])
    gp = gp_ref[...]
    glog = -gp[0:1, :] * _softplus(gates + gp[1:2, :])
    beta = _sigmoid(gates)

    ph_d = _dot(hh, w_ref[:, OFF_DN:OFF_DN + 3 * DN_W])
    ph_a = _dot(hh, w_ref[:, CONV_CH:OFF_ATT])
    pd = _dot(h, w_ref[:, OFF_DN:OFF_DN + 3 * DN_W])

    lane = _iota2((CUM_ROWS, LANES), 1)
    for r0 in range(0, tm, CUM_ROWS):
        g = glog[r0:r0 + CUM_ROWS]
        pre = _dot_exact_rhs(cum_ref[...], g)
        tot = jnp.concatenate([jnp.broadcast_to(pre[c + CHUNK - 1:c + CHUNK], (CHUNK, LANES))
                               for c in range(0, CUM_ROWS, CHUNK)], axis=0)
        o_g[0, r0:r0 + CUM_ROWS, :] = jnp.where(
            lane < DN_HEADS, pre, jnp.where(lane < 2 * DN_HEADS, tot - pre + g, beta[r0:r0 + CUM_ROWS]))

    dn_ext[0:F32_SUBLANES, :] = ph_d[:F32_SUBLANES] * has_prev
    dn_ext[F32_SUBLANES:F32_SUBLANES + tm, :] = pd
    dn_ext[F32_SUBLANES + tm:, :] = ph_d[F32_SUBLANES:] * has_next
    ones_bd = bdb_ref[...]
    grp = tm // INPROJ_POST_GROUPS

    def dn_post(r0):
        mid = dn_ext[F32_SUBLANES + r0:F32_SUBLANES + r0 + grp, :]
        qkv = _silu(_conv3(dn_ext.at[pl.ds(r0, grp + 2 * F32_SUBLANES)], mid, cwd_ref[...], grp))
        q = qkv[:, :DN_W]
        k = qkv[:, DN_W:2 * DN_W]
        rows = pl.ds(r0, grp)
        o_dn[0, rows, :DN_W] = (q * (lax.rsqrt(_head_sumsq(q, ones_bd) + RMS_EPS) * (DN_DIM ** -0.5))).astype(BF16)
        o_dn[0, rows, DN_W:2 * DN_W] = (k * lax.rsqrt(_head_sumsq(k, ones_bd) + RMS_EPS)).astype(BF16)
        o_dn[0, rows, 2 * DN_W:] = qkv[:, 2 * DN_W:].astype(BF16)

    post = [functools.partial(dn_post, g * grp) for g in range(INPROJ_POST_GROUPS)]
    for f in post[:len(post) // 2]:
        f()
    pa = _dot(h, w_ref[:, :OFF_ATT])
    for f in post[len(post) // 2:len(post) * 3 // 4]:
        f()
    o_q[0] = (_dot(h, w_ref[:, OFF_ATT:OFF_ATT + ATT_Q]) * (HEAD_DIM ** -0.5 * LOG2E)).astype(BF16)
    for f in post[len(post) * 3 // 4:]:
        f()
    o_kv[0] = _dot(h, w_ref[:, OFF_ATT + ATT_Q:OFF_DN]).astype(BF16)

    cu = pa[:, CONV_CH:2 * CONV_CH] * pa[:, 2 * CONV_CH:]
    cuh = ph_a[:, :CONV_CH] * ph_a[:, CONV_CH:]
    cu_ext[0:F32_SUBLANES, :] = cuh[:F32_SUBLANES] * has_prev
    cu_ext[F32_SUBLANES:F32_SUBLANES + tm, :] = cu
    cu_ext[F32_SUBLANES + tm:, :] = cuh[F32_SUBLANES:] * has_next
    o_ya[0] = (pa[:, :CONV_CH] * _conv3(cu_ext, cu, cwa_ref[...], tm)).astype(BF16)

    o_z[0] = _dot(h, w_ref[:, OFF_DN + 3 * DN_W:OFF_GATE]).astype(BF16)


def _block_diag_ones():
    r = np.arange(DN_W)
    return jnp.asarray((r[:, None] // DN_DIM) == (r[None, :] // DN_DIM), BF16)


def _chunk_cumsum_selector():
    t = np.arange(CUM_ROWS)
    same_chunk = (t[:, None] // CHUNK) == (t[None, :] // CHUNK)
    return jnp.asarray(same_chunk & (t[None, :] <= t[:, None]), BF16)


def _in_projection(x, mod_l, b0, ln_g, ln_b, w_main, w_gate, conv_a_w, dn_conv_w, gate_params, apply_ln):
    bsz, s, _ = x.shape
    tm = min(INPROJ_TILE, s)
    assert s % tm == 0 and tm % CUM_ROWS == 0, (s, tm)
    nh = tm // F32_SUBLANES
    row = lambda b, i: (b, i, 0)
    full2 = lambda b, i: (0, 0)
    const = lambda shape: pl.BlockSpec(shape, full2, pipeline_mode=pl.Buffered(1))
    widths = (CONV_CH, ATT_Q, 2 * ATT_KV, 3 * DN_W, DN_W, LANES)
    dtypes = (BF16, BF16, BF16, BF16, BF16, F32)
    out_shape = [jax.ShapeDtypeStruct((bsz, s, w), dt) for w, dt in zip(widths, dtypes)]
    out_specs = [pl.BlockSpec((1, tm, w), row) for w in widths]
    if apply_ln:
        out_shape.append(jax.ShapeDtypeStruct((bsz, s, D_MODEL), F32))
        out_specs.append(pl.BlockSpec((1, tm, D_MODEL), row))
    return pl.pallas_call(
        functools.partial(_inproj_kernel, apply_ln),
        grid=(bsz, s // tm),
        in_specs=[
            pl.BlockSpec((1, tm, D_MODEL), row),
            pl.BlockSpec((1, F32_SUBLANES, D_MODEL), lambda b, i: (b, jnp.maximum(i * nh - 1, 0), 0)),
            pl.BlockSpec((1, F32_SUBLANES, D_MODEL),
                         lambda b, i: (b, jnp.minimum((i + 1) * nh, s // F32_SUBLANES - 1), 0)),
            pl.BlockSpec((1, 1, D_MODEL), lambda b, i: (b0 + b, 0, 0)),
            pl.BlockSpec((1, 1, D_MODEL), lambda b, i: (b0 + b, 0, 1)),
            pl.BlockSpec((1, D_MODEL), full2),
            pl.BlockSpec((1, D_MODEL), full2),
            const((D_MODEL, OFF_GATE)),
            const((D_MODEL, LANES)),
            const((F32_SUBLANES, CONV_CH)),
            const((F32_SUBLANES, 3 * DN_W)),
            const((DN_W, DN_W)),
            const((F32_SUBLANES, LANES)),
            const((CUM_ROWS, CUM_ROWS)),
        ],
        out_specs=out_specs,
        out_shape=out_shape,
        scratch_shapes=[pltpu.VMEM((tm + 2 * F32_SUBLANES, CONV_CH), F32),
                        pltpu.VMEM((tm + 2 * F32_SUBLANES, 3 * DN_W), F32)],
        compiler_params=pltpu.CompilerParams(dimension_semantics=("parallel", "parallel"),
                                             vmem_limit_bytes=VMEM_LIMIT_BYTES),
        name="in_projection",
    )(x, x, x, mod_l, mod_l, ln_g, ln_b, w_main, w_gate, conv_a_w, dn_conv_w, _block_diag_ones(), gate_params,
      _chunk_cumsum_selector())


def _attention_bias():
    qi = jnp.arange(ATT_BLOCK)[:, None]
    ki = jnp.arange(3 * ATT_BLOCK)[None, :]
    dist = jnp.abs(ki - ATT_BLOCK - qi)
    slopes = jnp.exp2(-8.0 * jnp.arange(1, N_HEADS + 1, dtype=F32) / N_HEADS)
    base = -(slopes * LOG2E)[:, None, None] * dist.astype(F32)[None]
    variants = []
    for var in range(4):
        valid = dist <= WINDOW
        if var & 1:
            valid = valid & (ki >= ATT_BLOCK)
        if var & 2:
            valid = valid & (ki < 2 * ATT_BLOCK)
        variants.append(jnp.where(valid[None], base, -jnp.inf))
    return jnp.stack(variants)


def _attention_kernel(q_ref, kv_ref, kv_prev_ref, kv_next_ref, bias_ref, sink_ref, o_ref,
                      ka_ref, kb_ref, va_ref, vb_ref):
    i = pl.program_id(1)
    n = pl.num_programs(1)
    tq = q_ref.shape[1]

    def fill(r0, blk):
        kk = blk[:, :LANES].astype(F32)
        vv = blk[:, LANES:].astype(F32)
        lo = _iota2(kk.shape, 1) < HEAD_DIM
        kks = pltpu.roll(kk, HEAD_DIM, 1)
        vvs = pltpu.roll(vv, HEAD_DIM, 1)
        rows = pl.ds(r0, blk.shape[0])
        ka_ref[0, rows, :] = jnp.where(lo, kk, 0.0).astype(BF16)
        kb_ref[0, rows, :] = jnp.where(lo, 0.0, kks).astype(BF16)
        ka_ref[1, rows, :] = jnp.where(lo, kks, 0.0).astype(BF16)
        kb_ref[1, rows, :] = jnp.where(lo, 0.0, kk).astype(BF16)
        va_ref[0, rows, :] = jnp.where(lo, vv, 0.0).astype(BF16)
        vb_ref[0, rows, :] = jnp.where(lo, 0.0, vvs).astype(BF16)
        va_ref[1, rows, :] = jnp.where(lo, vvs, 0.0).astype(BF16)
        vb_ref[1, rows, :] = jnp.where(lo, 0.0, vv).astype(BF16)

    fill(0, kv_prev_ref[0])
    fill(ATT_BLOCK, kv_ref[0])
    fill(ATT_BLOCK + tq, kv_next_ref[0])

    nwin = 3 * ATT_BLOCK
    nsub = tq // ATT_BLOCK
    lane_lo = _iota2((ATT_BLOCK, LANES), 1) < HEAD_DIM
    grp = N_HEADS // N_KV_HEADS

    units = [(j, hk) for j in range(nsub) for hk in range(N_KV_HEADS)]

    def scores(j, hk):
        blk = i * nsub + j
        var = jnp.where(blk == 0, 1, 0) + jnp.where(blk == n * nsub - 1, 2, 0)
        win = pl.ds(j * ATT_BLOCK, nwin)
        qrows = pl.ds(j * ATT_BLOCK, ATT_BLOCK)
        out = []
        for h in range(hk * grp, (hk + 1) * grp):
            qp = q_ref[0, qrows, (h // 2) * LANES:(h // 2 + 1) * LANES]
            kz = (kb_ref if h % 2 else ka_ref)[hk, win, :]
            out.append(_dot_nt(qp, kz) + bias_ref[var, h])
        return out

    sc = scores(*units[0])
    for u, (j, hk) in enumerate(units):
        sc_next = scores(*units[u + 1]) if u + 1 < len(units) else None
        win = pl.ds(j * ATT_BLOCK, nwin)
        qrows = pl.ds(j * ATT_BLOCK, ATT_BLOCK)
        hs = range(hk * grp, (hk + 1) * grp)
        m = [jnp.maximum(jnp.max(s_, axis=-1, keepdims=True), sink_ref[h]) for s_, h in zip(sc, hs)]
        e = [jnp.exp2(s_ - m_) for s_, m_ in zip(sc, m)]
        rcp = [1.0 / (jnp.sum(e_, axis=-1, keepdims=True) + jnp.exp2(sink_ref[h] - m_))
               for e_, m_, h in zip(e, m, hs)]
        pv = [_dot(e_.astype(BF16), (vb_ref if h % 2 else va_ref)[hk, win, :]) for e_, h in zip(e, hs)]
        for t in range(grp // 2):
            pair = hk * (grp // 2) + t
            out = (pv[2 * t] + pv[2 * t + 1]) * jnp.where(lane_lo, rcp[2 * t], rcp[2 * t + 1])
            o_ref[0, qrows, pair * LANES:(pair + 1) * LANES] = out.astype(BF16)
        sc = sc_next


def _attention(q, kv, bias, sink_log2):
    bsz, s, _ = q.shape
    tq = min(ATT_TILE, s)
    nb = tq // ATT_BLOCK
    row = lambda b, i: (b, i, 0)
    return pl.pallas_call(
        _attention_kernel,
        grid=(bsz, s // tq),
        in_specs=[
            pl.BlockSpec((1, tq, ATT_Q), row),
            pl.BlockSpec((1, tq, 2 * ATT_KV), row),
            pl.BlockSpec((1, ATT_BLOCK, 2 * ATT_KV), lambda b, i: (b, jnp.maximum(i * nb - 1, 0), 0)),
            pl.BlockSpec((1, ATT_BLOCK, 2 * ATT_KV),
                         lambda b, i: (b, jnp.minimum((i + 1) * nb, s // ATT_BLOCK - 1), 0)),
            pl.BlockSpec(bias.shape, lambda b, i: (0, 0, 0, 0), pipeline_mode=pl.Buffered(1)),
            pl.BlockSpec(memory_space=pltpu.SMEM),
        ],
        out_specs=pl.BlockSpec((1, tq, ATT_Q), row),
        out_shape=jax.ShapeDtypeStruct((bsz, s, ATT_Q), BF16),
        scratch_shapes=[
            pltpu.VMEM((N_KV_HEADS, tq + 2 * ATT_BLOCK, LANES), BF16),
            pltpu.VMEM((N_KV_HEADS, tq + 2 * ATT_BLOCK, LANES), BF16),
            pltpu.VMEM((N_KV_HEADS, tq + 2 * ATT_BLOCK, LANES), BF16),
            pltpu.VMEM((N_KV_HEADS, tq + 2 * ATT_BLOCK, LANES), BF16),
        ],
        compiler_params=pltpu.CompilerParams(dimension_semantics=("parallel", "parallel"),
                                             vmem_limit_bytes=VMEM_LIMIT_BYTES),
        name="window_attention",
    )(q, kv, kv, kv, bias, sink_log2)


N_LEVELS = 6
CM_EYE, CM_NEG, CM_LEVEL0 = 0, 1, 2


def _dn_constants(reverse, ts):
    i = np.arange(CHUNK)[:, None]
    j = (np.arange(DN_W) % DN_DIM)[None, :]
    tri_incl = (i <= j) if reverse else (i >= j)
    rows = [(i == j).astype(np.float32), np.where(tri_incl, 0.0, -np.inf).astype(np.float32)]
    for lvl in range(1, N_LEVELS + 1):
        half = 1 << (lvl - 1)
        same_blk = (i >> lvl) == (j >> lvl)
        i_hi, j_hi = (i & half) != 0, (j & half) != 0
        rows.append((same_blk & ~i_hi & j_hi if reverse else same_blk & i_hi & ~j_hi).astype(np.float32))
    chunk_masks = np.stack(rows)
    r = np.arange(DN_W)
    mask_bd = (r[:, None] // DN_DIM) == (r[None, :] // DN_DIM)
    bd_f32 = np.stack([mask_bd.astype(np.float32), np.eye(DN_W, dtype=np.float32)])
    goff, boff = (DN_HEADS, 3 * DN_HEADS) if reverse else (0, 2 * DN_HEADS)
    er = np.arange(LANES)[:, None]
    ec = (np.arange(DN_W) // DN_DIM)[None, :]
    expand = np.stack([er == ec + goff, er == ec + boff])
    return (jnp.asarray(chunk_masks), jnp.asarray(chunk_masks[CM_LEVEL0:], BF16), jnp.asarray(bd_f32),
            jnp.asarray(mask_bd, BF16), jnp.asarray(expand, BF16))


def _bd(y, bdb_ref):
    return jnp.concatenate([y] * DN_HEADS, axis=0) * bdb_ref[...]


def _dn_prep(reverse, x_ref, gfeat, cm_ref, cmb_ref, bdf_ref, bdb_ref, exp_ref):
    nchunk = x_ref.shape[1] // CHUNK
    cs = range(nchunk)
    rows = [slice(c * CHUNK, (c + 1) * CHUNK) for c in cs]
    bd = functools.partial(_bd, bdb_ref=bdb_ref)

    kb16 = [x_ref[0, r, DN_W:2 * DN_W] for r in rows]
    gq = [_dot_nt(jnp.concatenate([kb, x_ref[0, r, :DN_W]], axis=0), bd(kb)) for kb, r in zip(kb16, rows)]
    yield
    q = x_ref[0, :, :DN_W].astype(F32)
    k = x_ref[0, :, DN_W:2 * DN_W].astype(F32)
    v = x_ref[0, :, 2 * DN_W:].astype(F32)
    g_hi, g_mid, g_lo = _split3(gfeat)
    gcol = _dot(g_hi, exp_ref[0]) + _dot(g_mid, exp_ref[0]) + _dot(g_lo, exp_ref[0])
    bcol = _dot(g_hi, exp_ref[1]) + _dot(g_mid, exp_ref[1]) + _dot(g_lo, exp_ref[1])
    yield
    eye_p = cm_ref[CM_EYE]
    egc = jnp.exp(gcol)
    kbg = k * (bcol * egc)
    vb = v * bcol
    qe = q * egc
    gcs = [gcol[r] for r in rows]
    grow = [jnp.sum(eye_p * g, axis=0, keepdims=True) for g in gcs]
    gtot = [g[0:1, :] if reverse else g[CHUNK - 1:CHUNK, :] for g in gcs]
    decay = [jnp.exp(g - gr + cm_ref[CM_NEG]) for g, gr in zip(gcs, grow)]
    lb = [g[:CHUNK] * bcol[r] * d for g, r, d in zip(gq, rows, decay)]
    ab = [(g[CHUNK:] * d).astype(BF16) for g, d in zip(gq, decay)]
    x = [eye_p - cm_ref[CM_LEVEL0] * l for l in lb]
    lb16 = [l.astype(BF16) for l in lb]
    for lvl in range(1, N_LEVELS):
        xb = [xc.astype(BF16) for xc in x]
        z = [_dot(cmb_ref[lvl] * l, bd(b)) for l, b in zip(lb16, xb)]
        yield
        x = [xc - _dot(b, bd(zc.astype(BF16))) for xc, b, zc in zip(x, xb, z)]
        yield
    xb = [xc.astype(BF16) for xc in x]
    u = [_dot(b, bd(vb[r].astype(BF16))) for b, r in zip(xb, rows)]
    yield
    w = [_dot(b, bd(kbg[r].astype(BF16))) for b, r in zip(xb, rows)]
    yield
    p = [(qe[r] - _dot(a, bd(wc.astype(BF16)))).astype(BF16) for r, a, wc in zip(rows, ab, w)]
    yield
    rr = [_dot(a, bd(uc.astype(BF16))) for a, uc in zip(ab, u)]
    yield
    kd = [(k[r] * jnp.exp(gt - g)).astype(BF16) for r, gt, g in zip(rows, gtot, gcs)]
    full = [_dot_tn(jnp.concatenate([wc, uc], axis=1).astype(BF16), kdc) for wc, uc, kdc in zip(w, u, kd)]
    yield
    bdm = [(bdf_ref[1] * jnp.exp(gt) - bdf_ref[0] * f[:DN_W]).astype(BF16) for gt, f in zip(gtot, full)]
    nfull = [bdf_ref[0] * f[DN_W:] for f in full]
    nt = [nf[0:64] + nf[64:128] + nf[128:192] + nf[192:256] for nf in nfull]
    return p, rr, bdm, nt


def _dn_kernel(reverse, n_tiles, x_ref, g_ref, cm_ref, cmb_ref, bdf_ref, bdb_ref, exp_ref, *rest):
    if reverse:
        z_ref, of_ref, ng_ref, o_ref, st_ref, p_scr, r_scr, m_scr, n_scr = rest
    else:
        o_ref, st_ref, p_scr, r_scr, m_scr, n_scr = rest
    s = pl.program_id(0)
    nchunk = x_ref.shape[1] // CHUNK

    @pl.when(s == 0)
    def _():
        st_ref[...] = jnp.zeros(st_ref.shape, F32)
        p_scr[1] = jnp.zeros(p_scr.shape[1:], BF16)
        r_scr[1] = jnp.zeros(r_scr.shape[1:], F32)
        m_scr[1] = jnp.zeros(m_scr.shape[1:], BF16)
        n_scr[1] = jnp.zeros(n_scr.shape[1:], F32)

    cur = lax.rem(s, 2)
    prev = 1 - cur
    scanned = jnp.maximum(s - 1, 0)
    first = lax.rem(scanned, n_tiles) == 0
    st = jnp.where(first, 0.0, st_ref[...])

    prep = _dn_prep(reverse, x_ref, g_ref[0], cm_ref, cmb_ref, bdf_ref, bdb_ref, exp_ref)

    def scan_step(c, st):
        rows = pl.ds(c * CHUNK, CHUNK)
        sb = st.astype(BF16)
        st_new = _dot(sb, m_scr[prev, c]) + n_scr[prev, rows, :]
        o = _dot_nt(p_scr[prev, rows, :], _bd(sb, bdb_ref)) + r_scr[prev, rows, :]
        return st_new, (rows, o)

    def finish(pending):
        rows, o = pending
        if reverse:
            o = o + of_ref[0, rows, :]
            ms = _head_sumsq(o, bdb_ref[...]) * (1.0 / DN_DIM)
            y = o * lax.rsqrt(ms + RMS_EPS) * ng_ref[...] * _silu(z_ref[0, rows, :].astype(F32))
            o_ref[0, rows, :] = y.astype(BF16)
        else:
            o_ref[0, rows, :] = o

    order = list(reversed(range(nchunk))) if reverse else list(range(nchunk))
    groups = 0
    pending = None
    while True:
        try:
            next(prep)
        except StopIteration as done:
            p, rr, bdm, nt = done.value
            break
        groups += 1
        if groups % 2 == 0 and order:
            if pending is not None:
                finish(pending)
            st, pending = scan_step(order.pop(0), st)
    while order:
        finish(pending)
        st, pending = scan_step(order.pop(0), st)
    finish(pending)
    st_ref[...] = st
    for c in range(nchunk):
        rows = pl.ds(c * CHUNK, CHUNK)
        p_scr[cur, rows, :] = p[c]
        r_scr[cur, rows, :] = rr[c]
        m_scr[cur, c] = bdm[c]
        n_scr[cur, rows, :] = nt[c]


def _delta_pass(reverse, dnqkv, gfeat, z=None, o_fwd=None, norm_g=None):
    bsz, s, _ = dnqkv.shape
    ts = min(DN_TILE, s)
    n = s // ts
    total = bsz * n
    pos = (lambda i: n - 1 - i) if reverse else (lambda i: i)

    def tile_of(flat):
        return flat // n, pos(lax.rem(flat, n)), 0

    prepared = lambda s_: tile_of(jnp.minimum(s_, total - 1))
    scanned = lambda s_: tile_of(jnp.maximum(s_ - 1, 0))
    in_specs = [
        pl.BlockSpec((1, ts, 3 * DN_W), prepared),
        pl.BlockSpec((1, ts, LANES), prepared),
    ]
    consts = _dn_constants(reverse, ts)
    in_specs += [pl.BlockSpec(c.shape, (lambda nd: lambda s_: (0,) * nd)(c.ndim), pipeline_mode=pl.Buffered(1))
                 for c in consts]
    args = [dnqkv, gfeat, *consts]
    if reverse:
        in_specs += [pl.BlockSpec((1, ts, DN_W), scanned), pl.BlockSpec((1, ts, DN_W), scanned),
                     pl.BlockSpec((1, DN_W), lambda s_: (0, 0))]
        args += [z, o_fwd, norm_g]
        out_dtype = BF16
    else:
        out_dtype = F32
    return pl.pallas_call(
        functools.partial(_dn_kernel, reverse, n),
        grid=(total + 1,),
        in_specs=in_specs,
        out_specs=pl.BlockSpec((1, ts, DN_W), scanned),
        out_shape=jax.ShapeDtypeStruct((bsz, s, DN_W), out_dtype),
        scratch_shapes=[pltpu.VMEM((DN_DIM, DN_W), F32),
                        pltpu.VMEM((2, ts, DN_W), BF16),
                        pltpu.VMEM((2, ts, DN_W), F32),
                        pltpu.VMEM((2, ts // CHUNK, DN_W, DN_W), BF16),
                        pltpu.VMEM((2, ts, DN_W), F32)],
        compiler_params=pltpu.CompilerParams(dimension_semantics=("arbitrary",),
                                             vmem_limit_bytes=VMEM_LIMIT_BYTES),
        name="delta_bwd" if reverse else "delta_fwd",
    )(*args)


def _out_mlp_kernel(x_ref, ya_ref, yb_ref, yc_ref, g1_ref, sh2_ref, sc2_ref, g2_ref, wo_a_ref, wo_b_ref,
                    wo_c_ref, ln1g_ref, ln1b_ref, w1_ref, b1_ref, w2_ref, b2_ref, ln2g_ref, ln2b_ref, o_ref):
    tm = x_ref.shape[1]
    groups = [pl.ds(g * (tm // MLP_ROW_GROUPS), tm // MLP_ROW_GROUPS) for g in range(MLP_ROW_GROUPS)]
    y = [_dot(ya_ref[0, r, :], wo_a_ref[...]) + _dot(yb_ref[0, r, :], wo_b_ref[...])
         + _dot(yc_ref[0, r, :], wo_c_ref[...]) for r in groups]
    x1 = [_layer_norm(ALPHA * x_ref[0, r, :] + (1.0 + g1_ref[0]) * yg, ln1g_ref[...], ln1b_ref[...])
          for r, yg in zip(groups, y)]
    h = [(xg * (1.0 + sc2_ref[0]) + sh2_ref[0]).astype(BF16) for xg in x1]
    f = [None] * MLP_ROW_GROUPS
    for c in range(D_FF // FF_CHUNK):
        cols = slice(c * FF_CHUNK, (c + 1) * FF_CHUNK)
        a = [jnp.maximum(_dot(hg, w1_ref[:, cols]) + b1_ref[:, cols], 0.0) for hg in h]
        part = [_dot((ag * ag).astype(BF16), w2_ref[cols, :]) for ag in a]
        f = [pg if fg is None else fg + pg for fg, pg in zip(f, part)]
    for r, xg, fg in zip(groups, x1, f):
        o_ref[0, r, :] = _layer_norm(ALPHA * xg + (1.0 + g2_ref[0]) * (fg + b2_ref[...]),
                                     ln2g_ref[...], ln2b_ref[...])


def _out_mlp(x, ya, yb, yc, mod_l, b0, wo_a, wo_b, wo_c, ln1g, ln1b, w1, b1, w2, b2, ln2g, ln2b):
    bsz, s, _ = x.shape
    tm = min(ROW_TILE, s)
    row = lambda b, i: (b, i, 0)
    full2 = lambda b, i: (0, 0)
    modspec = lambda k: pl.BlockSpec((1, 1, D_MODEL), lambda b, i: (b0 + b, 0, k))
    const = lambda shape: pl.BlockSpec(shape, full2, pipeline_mode=pl.Buffered(1))
    return pl.pallas_call(
        _out_mlp_kernel,
        grid=(bsz, s // tm),
        in_specs=[
            pl.BlockSpec((1, tm, D_MODEL), row),
            pl.BlockSpec((1, tm, CONV_CH), row),
            pl.BlockSpec((1, tm, ATT_Q), row),
            pl.BlockSpec((1, tm, DN_W), row),
            modspec(2), modspec(3), modspec(4), modspec(5),
            const((CONV_CH, D_MODEL)), const((ATT_Q, D_MODEL)), const((DN_W, D_MODEL)),
            const((1, D_MODEL)), const((1, D_MODEL)),
            const((D_MODEL, D_FF)), const((1, D_FF)),
            const((D_FF, D_MODEL)), const((1, D_MODEL)),
            const((1, D_MODEL)), const((1, D_MODEL)),
        ],
        out_specs=pl.BlockSpec((1, tm, D_MODEL), row),
        out_shape=jax.ShapeDtypeStruct((bsz, s, D_MODEL), F32),
        compiler_params=pltpu.CompilerParams(dimension_semantics=("parallel", "parallel"),
                                             vmem_limit_bytes=VMEM_LIMIT_BYTES),
        name="out_mlp",
    )(x, ya, yb, yc, mod_l, mod_l, mod_l, mod_l, wo_a, wo_b, wo_c, ln1g, ln1b, w1, b1, w2, b2, ln2g, ln2b)


def _pad_rows(a, rows):
    return jnp.concatenate([a, jnp.zeros((rows - a.shape[0],) + a.shape[1:], a.dtype)], axis=0)


def _trunk(x, mod, b0, prm):
    for l in range(DEPTH):
        mod_l = mod[l]
        outs = _in_projection(x, mod_l, b0, prm["ln_in_g"], prm["ln_in_b"], prm["w_main"][l], prm["w_gate"][l],
                              prm["conv_a_w"][l], prm["dn_conv_w"][l], prm["gate_params"][l], apply_ln=(l == 0))
        ya, q, kv, dnqkv, z, gfeat = outs[:6]
        if l == 0:
            x = outs[6]
        yb = _attention(q, kv, prm["attn_bias"], prm["sink_log2"][l])
        o_fwd = _delta_pass(False, dnqkv, gfeat)
        yc = _delta_pass(True, dnqkv, gfeat, z=z, o_fwd=o_fwd, norm_g=prm["dn_norm_g"][l])
        x = _out_mlp(x, ya, yb, yc, mod_l, b0, prm["wo_a"][l], prm["wo_b"][l], prm["wo_c"][l],
                     prm["ln1_g"][l], prm["ln1_b"][l], prm["w1"][l], prm["b1"][l], prm["w2"][l], prm["b2"][l],
                     prm["ln2_g"][l], prm["ln2_b"][l])
    return x


def kernel(x_prompt, x_sample, c_prompt, c_sample, ln_in_g, ln_in_b, w_mod, b_mod, w_in, conv_a_w, attn_sink,
           dn_conv_w, dn_a_log_f, dn_a_log_b, dn_dt_bias_f, dn_dt_bias_b, dn_norm_g, w_out, ln1_g, ln1_b,
           w1, b1, w2, b2, ln2_g, ln2_b):
    nb_p, nb_s = c_prompt.shape[0], c_sample.shape[0]
    bp = -(-(nb_p + nb_s) // BF16_SUBLANES) * BF16_SUBLANES
    c_all = _pad_rows(jnp.concatenate([c_prompt, c_sample], axis=0), bp)
    mod = _modulation(c_all, w_mod, b_mod)
    mod = mod.reshape(DEPTH, bp, 1, 6 * D_MODEL)

    row = lambda a: a.reshape(DEPTH, 1, -1)
    zeros4 = jnp.zeros((DEPTH, DN_HEADS), F32)
    gp0 = jnp.concatenate([jnp.exp(dn_a_log_f), jnp.exp(dn_a_log_b), zeros4, zeros4], axis=1)
    gp1 = jnp.concatenate([dn_dt_bias_f, dn_dt_bias_b, zeros4, zeros4], axis=1)
    gate_params = jnp.stack([gp0, gp1], axis=1)
    gate_params = jnp.pad(gate_params, ((0, 0), (0, F32_SUBLANES - 2), (0, LANES - 4 * DN_HEADS)))
    pad_taps = lambda w: jnp.pad(w, ((0, 0), (0, F32_SUBLANES - w.shape[1]), (0, 0)))
    prm = dict(
        ln_in_g=ln_in_g.reshape(1, -1), ln_in_b=ln_in_b.reshape(1, -1),
        w_main=w_in[:, :, :OFF_GATE].astype(BF16),
        w_gate=jnp.pad(w_in[:, :, OFF_GATE:], ((0, 0), (0, 0), (0, LANES - 4 * DN_HEADS))).astype(BF16),
        conv_a_w=pad_taps(conv_a_w), attn_bias=_attention_bias(), sink_log2=attn_sink * LOG2E,
        dn_conv_w=pad_taps(dn_conv_w), gate_params=gate_params,
        dn_norm_g=jnp.tile(dn_norm_g, (1, DN_HEADS)).reshape(DEPTH, 1, DN_W),
        wo_a=w_out[:, :CONV_CH, :].astype(BF16), wo_b=w_out[:, CONV_CH:CONV_CH + ATT_Q, :].astype(BF16),
        wo_c=w_out[:, CONV_CH + ATT_Q:, :].astype(BF16),
        ln1_g=row(ln1_g), ln1_b=row(ln1_b), w1=w1.astype(BF16), b1=row(b1), w2=w2.astype(BF16), b2=row(b2),
        ln2_g=row(ln2_g), ln2_b=row(ln2_b),
    )
    y_prompt = _trunk(x_prompt, mod, 0, prm)
    y_sample = _trunk(x_sample, mod, nb_p, prm)
    return (y_prompt, y_sample)
```
